```python
import jax, jax.numpy as jnp
from jax import lax
import numpy as np

D_MODEL = 4096
BATCH = 4
SEQ = 4096
DEPTH = 2

MIX_WIDTH = D_MODEL
PLE_DIM = 256
RMS_EPS = 1e-6

RET_HEAD_DIM = 128
RET_WIDTH = MIX_WIDTH // 4
RET_HEADS = RET_WIDTH // RET_HEAD_DIM
RET_CHUNK = 128
ROPE_BASE = 10000.0
RET_GN_EPS = 1e-5

RWKV_HEAD_DIM = 64
RWKV_WIDTH = MIX_WIDTH // 4
RWKV_HEADS = RWKV_WIDTH // RWKV_HEAD_DIM
RWKV_LORA = 64
RWKV_GN_EPS = 64e-5

GDN_HEAD_DIM = 128
GDN_WIDTH = MIX_WIDTH - RET_WIDTH - RWKV_WIDTH
GDN_HEADS = GDN_WIDTH // GDN_HEAD_DIM
GDN_CONV = 4
GDN_CHUNK = 64

RET_COLS = 4 * RET_WIDTH
RWKV_COLS = 4 * RWKV_WIDTH + 2 * RWKV_LORA
GDN_COLS = 4 * GDN_WIDTH + 2 * GDN_HEADS
IN_WIDTH = RET_COLS + RWKV_COLS + GDN_COLS

kernel_name = "hybrid_retention_rwkv7_gdn_block"


def rms_norm(x, w):
    x32 = x.astype(jnp.float32)
    y = x32 * lax.rsqrt(jnp.mean(x32 * x32, axis=-1, keepdims=True) + RMS_EPS)
    return (y * w.astype(jnp.float32)).astype(x.dtype)


def head_standardize(y, eps):
    yc = y - jnp.mean(y, axis=-1, keepdims=True)
    return yc * lax.rsqrt(jnp.mean(yc * yc, axis=-1, keepdims=True) + eps)


def l2_normalize(t, eps=1e-6):
    return t * lax.rsqrt(jnp.sum(t * t, axis=-1, keepdims=True) + eps)


def token_shift(u):
    return jnp.pad(u, ((0, 0), (1, 0), (0, 0)))[:, :-1]


def apply_rope(t, positions):
    half = t.shape[-1] // 2
    inv_freq = ROPE_BASE ** (-jnp.arange(half, dtype=jnp.float32) / half)
    ang = positions.astype(jnp.float32)[..., None] * inv_freq
    cos = jnp.cos(ang)[:, :, None, :]
    sin = jnp.sin(ang)[:, :, None, :]
    t1, t2 = t[..., :half], t[..., half:]
    return jnp.concatenate([t1 * cos - t2 * sin, t2 * cos + t1 * sin], axis=-1)


def retention_chunked(q, k, v):
    b, s, h, dk = q.shape
    dv = v.shape[-1]
    c = RET_CHUNK
    n = s // c
    log_gamma = jnp.log1p(-(2.0 ** (-5.0 - jnp.arange(h, dtype=jnp.float32))))
    idx = jnp.arange(c, dtype=jnp.float32)
    rel = idx[:, None] - idx[None, :]
    mask = jnp.where(rel >= 0, jnp.exp(jnp.maximum(rel, 0.0)[None] * log_gamma[:, None, None]), 0.0)
    xi = jnp.exp((idx + 1.0)[:, None] * log_gamma[None, :])
    zeta = jnp.exp((c - 1.0 - idx)[:, None] * log_gamma[None, :])
    chunk_decay = jnp.exp(c * log_gamma)
    qc = q.reshape(b, n, c, h, dk)
    kc = k.reshape(b, n, c, h, dk)
    vc = v.reshape(b, n, c, h, dv)
    scores = jnp.einsum('bnihd,bnjhd->bnhij', qc, kc) * mask
    inner = jnp.einsum('bnhij,bnjhe->bnihe', scores, vc)
    kv = jnp.einsum('bnjhd,bnjhe->nbhde', kc * zeta[None, None, :, :, None], vc)

    def step(state, kv_n):
        return state * chunk_decay[None, :, None, None] + kv_n, state

    _, state_in = lax.scan(step, jnp.zeros((b, h, dk, dv), jnp.float32), kv)
    cross = jnp.einsum('bnihd,nbhde->bnihe', qc * xi[None, None, :, :, None], state_in)
    return (inner + cross).reshape(b, s, h, dv)


def retention_branch(seg, positions, gn_w):
    b, s, _ = seg.shape
    q, k, v, z = jnp.split(seg, 4, axis=-1)
    shp = (b, s, RET_HEADS, RET_HEAD_DIM)
    q = apply_rope(q.reshape(shp), positions)
    k = apply_rope(k.reshape(shp), positions) * (RET_HEAD_DIM ** -0.5)
    y = retention_chunked(q, k, v.reshape(shp))
    y = head_standardize(y, RET_GN_EPS).reshape(b, s, RET_WIDTH) * gn_w
    return y * jax.nn.silu(z)


def rwkv7_scan(r, decay, k, v, kk, kka):
    b, s, h, n = r.shape
    xs = tuple(jnp.moveaxis(t, 1, 0) for t in (r, decay, k, v, kk, kka))

    def step(state, inp):
        r_t, w_t, k_t, v_t, kk_t, kka_t = inp
        sa = jnp.einsum('bhvk,bhk->bhv', state, kk_t)
        state = (state * w_t[:, :, None, :]
                 - sa[..., None] * kka_t[:, :, None, :]
                 + v_t[..., None] * k_t[:, :, None, :])
        return state, jnp.einsum('bhvk,bhk->bhv', state, r_t)

    _, y = lax.scan(step, jnp.zeros((b, h, n, n), jnp.float32), xs)
    return jnp.moveaxis(y, 0, 1)


def rwkv7_branch(seg, mu, w0, w2, a0, a2, k_k, k_a, r_k, ln_w, ln_b):
    b, s, _ = seg.shape
    wd = RWKV_WIDTH
    seg = seg + (token_shift(seg) - seg) * mu
    r, k, v, z, dw, da = jnp.split(seg, [wd, 2 * wd, 3 * wd, 4 * wd, 4 * wd + RWKV_LORA], axis=-1)
    w_log = -jax.nn.softplus(-(w0 + jnp.matmul(jnp.tanh(dw), w2))) - 0.5
    decay = jnp.exp(-jnp.exp(w_log))
    a = jax.nn.sigmoid(a0 + jnp.matmul(da, a2))
    shp = (b, s, RWKV_HEADS, RWKV_HEAD_DIM)
    kk = l2_normalize((k * k_k).reshape(shp))
    k = k * (1.0 + (a - 1.0) * k_a)
    r_h, k_h, v_h = r.reshape(shp), k.reshape(shp), v.reshape(shp)
    y = rwkv7_scan(r_h, decay.reshape(shp), k_h, v_h, kk, kk * a.reshape(shp))
    y = head_standardize(y, RWKV_GN_EPS).reshape(b, s, wd) * ln_w + ln_b
    bonus = jnp.sum(r_h * k_h * r_k.reshape(RWKV_HEADS, RWKV_HEAD_DIM), axis=-1, keepdims=True) * v_h
    y = y + bonus.reshape(b, s, wd)
    return y * jax.nn.silu(z)


def causal_depthwise_conv(u, w):
    kw, ch = w.shape
    return lax.conv_general_dilated(u, w[:, None, :], window_strides=(1,), padding=[(kw - 1, 0)],
                                    dimension_numbers=('NWC', 'WIO', 'NWC'), feature_group_count=ch)


def gated_delta_chunked(q, k, v, g, beta):
    b, s, h, dk = q.shape
    dv = v.shape[-1]
    c = GDN_CHUNK
    n = s // c

    def chunks(t):
        return jnp.moveaxis(t.reshape((b, n, c, h) + t.shape[3:]), 3, 2)

    q, k, v, g, beta = chunks(q * (dk ** -0.5)), chunks(k), chunks(v), chunks(g), chunks(beta)
    gc = jnp.cumsum(g, axis=-1)
    causal = jnp.tril(jnp.ones((c, c), dtype=bool))
    strict = jnp.tril(jnp.ones((c, c), dtype=bool), -1)
    decay = jnp.exp(jnp.where(causal, gc[..., :, None] - gc[..., None, :], -jnp.inf))
    kb = k * beta[..., None]
    lower = jnp.where(strict, jnp.einsum('bnhid,bnhjd->bnhij', kb, k) * decay, 0.0)
    eye = jnp.eye(c, dtype=jnp.float32)
    rhs = jnp.concatenate([v * beta[..., None], kb * jnp.exp(gc)[..., None]], axis=-1)
    sol = lax.linalg.triangular_solve(eye + lower, rhs, left_side=True, lower=True, unit_diagonal=True)
    u, wk = sol[..., :dv], sol[..., dv:]
    attn = jnp.einsum('bnhid,bnhjd->bnhij', q, k) * decay
    q_dec = q * jnp.exp(gc)[..., None]
    k_tail = k * jnp.exp(gc[..., -1:] - gc)[..., None]
    last = jnp.exp(gc[..., -1])
    xs = tuple(jnp.moveaxis(t, 1, 0) for t in (q_dec, attn, u, wk, k_tail, last))

    def step(state, inp):
        qd, at, u_n, w_n, kt, ld = inp
        v_new = u_n - jnp.einsum('bhcd,bhde->bhce', w_n, state)
        o = jnp.einsum('bhcd,bhde->bhce', qd, state) + jnp.einsum('bhij,bhje->bhie', at, v_new)
        state = state * ld[..., None, None] + jnp.einsum('bhcd,bhce->bhde', kt, v_new)
        return state, o

    _, o = lax.scan(step, jnp.zeros((b, h, dk, dv), jnp.float32), xs)
    o = jnp.moveaxis(o, 0, 1)
    return jnp.moveaxis(o, 2, 3).reshape(b, s, h, dv)


def gdn_branch(seg, conv_w, a_log, dt_bias, norm_w):
    b, s, _ = seg.shape
    gw = GDN_WIDTH
    qkv, z, a, beta_logit = jnp.split(seg, [3 * gw, 4 * gw, 4 * gw + GDN_HEADS], axis=-1)
    qkv = jax.nn.silu(causal_depthwise_conv(qkv, conv_w.astype(jnp.float32)))
    q, k, v = jnp.split(qkv, 3, axis=-1)
    shp = (b, s, GDN_HEADS, GDN_HEAD_DIM)
    q = l2_normalize(q.reshape(shp))
    k = l2_normalize(k.reshape(shp))
    g = -jnp.exp(a_log.astype(jnp.float32)) * jax.nn.softplus(a + dt_bias)
    beta = jax.nn.sigmoid(beta_logit)
    o = gated_delta_chunked(q, k, v.reshape(shp), g, beta)
    o = o * lax.rsqrt(jnp.mean(o * o, axis=-1, keepdims=True) + RMS_EPS) * norm_w
    return o.reshape(b, s, gw) * jax.nn.silu(z)


def setup_inputs(seed: int = 0) -> dict:
    key = jax.random.key(seed)
    ks = jax.random.split(key, 26)
    f32 = jnp.float32

    def nrm(k, shape, scale):
        return jax.random.normal(k, shape, f32) * scale

    x = nrm(ks[0], (BATCH, SEQ, D_MODEL), 1.0)
    p = nrm(ks[1], (DEPTH, BATCH, SEQ, PLE_DIM), 1.0)
    offsets = jax.random.randint(ks[2], (BATCH, 1), 0, 1024, dtype=jnp.int32)
    positions = (offsets + jnp.arange(SEQ, dtype=jnp.int32)[None, :]).astype(jnp.int32)
    norm_w = 1.0 + nrm(ks[3], (DEPTH, D_MODEL), 0.02)
    w_in = nrm(ks[4], (DEPTH, D_MODEL, IN_WIDTH), D_MODEL ** -0.5)
    ret_gn = 1.0 + nrm(ks[5], (DEPTH, RET_WIDTH), 0.02)
    rwkv_mu = jax.random.uniform(ks[6], (DEPTH, RWKV_COLS), f32, 0.0, 1.0)
    ratio = jnp.arange(RWKV_WIDTH, dtype=f32) / (RWKV_WIDTH - 1)
    rwkv_w0 = (-7.0 + 5.0 * ratio ** 0.85 + 0.5)[None, :] + nrm(ks[7], (DEPTH, RWKV_WIDTH), 0.05)
    rwkv_w2 = nrm(ks[8], (DEPTH, RWKV_LORA, RWKV_WIDTH), 0.5 * RWKV_LORA ** -0.5)
    rwkv_a0 = nrm(ks[9], (DEPTH, RWKV_WIDTH), 0.1)
    rwkv_a2 = nrm(ks[10], (DEPTH, RWKV_LORA, RWKV_WIDTH), 0.5 * RWKV_LORA ** -0.5)
    rwkv_k_k = 0.85 + nrm(ks[11], (DEPTH, RWKV_WIDTH), 0.02)
    rwkv_k_a = 1.0 + nrm(ks[12], (DEPTH, RWKV_WIDTH), 0.02)
    rwkv_r_k = nrm(ks[13], (DEPTH, RWKV_WIDTH), 0.1)
    rwkv_ln_w = 1.0 + nrm(ks[14], (DEPTH, RWKV_WIDTH), 0.02)
    rwkv_ln_b = nrm(ks[15], (DEPTH, RWKV_WIDTH), 0.02)
    gdn_conv = nrm(ks[16], (DEPTH, GDN_CONV, 3 * GDN_WIDTH), GDN_CONV ** -0.5)
    gdn_a_log = jnp.log(jax.random.uniform(ks[17], (DEPTH, GDN_HEADS), f32, 1.0, 16.0))
    dt = jnp.exp(jax.random.uniform(ks[18], (DEPTH, GDN_HEADS), f32, float(np.log(1e-3)), float(np.log(1e-1))))
    gdn_dt_bias = dt + jnp.log(-jnp.expm1(-dt))
    gdn_norm = 1.0 + nrm(ks[19], (DEPTH, GDN_HEAD_DIM), 0.02)
    w_out = nrm(ks[20], (DEPTH, MIX_WIDTH, D_MODEL), MIX_WIDTH ** -0.5)
    w_ple = nrm(ks[21], (DEPTH, PLE_DIM, D_MODEL), PLE_DIM ** -0.5)
    ple_norm = 1.0 + nrm(ks[22], (DEPTH, D_MODEL), 0.02)
    w_ple_gate = nrm(ks[23], (DEPTH, D_MODEL, D_MODEL), D_MODEL ** -0.5)
    final_norm = 1.0 + nrm(ks[24], (D_MODEL,), 0.02)
    return {"x": x, "p": p, "positions": positions, "norm_w": norm_w, "w_in": w_in,
            "ret_gn": ret_gn, "rwkv_mu": rwkv_mu, "rwkv_w0": rwkv_w0, "rwkv_w2": rwkv_w2,
            "rwkv_a0": rwkv_a0, "rwkv_a2": rwkv_a2, "rwkv_k_k": rwkv_k_k, "rwkv_k_a": rwkv_k_a,
            "rwkv_r_k": rwkv_r_k, "rwkv_ln_w": rwkv_ln_w, "rwkv_ln_b": rwkv_ln_b,
            "gdn_conv": gdn_conv, "gdn_a_log": gdn_a_log, "gdn_dt_bias": gdn_dt_bias,
            "gdn_norm": gdn_norm, "w_out": w_out, "w_ple": w_ple, "ple_norm": ple_norm,
            "w_ple_gate": w_ple_gate, "final_norm": final_norm}


def reference(x, p, positions, norm_w, w_in, ret_gn, rwkv_mu, rwkv_w0, rwkv_w2, rwkv_a0, rwkv_a2,
              rwkv_k_k, rwkv_k_a, rwkv_r_k, rwkv_ln_w, rwkv_ln_b, gdn_conv, gdn_a_log, gdn_dt_bias,
              gdn_norm, w_out, w_ple, ple_norm, w_ple_gate, final_norm):
    for i in range(DEPTH):
        h = rms_norm(x, norm_w[i])
        proj = jnp.matmul(h, w_in[i]).astype(jnp.float32)
        seg_ret, seg_rwkv, seg_gdn = jnp.split(proj, [RET_COLS, RET_COLS + RWKV_COLS], axis=-1)
        y_ret = retention_branch(seg_ret, positions, ret_gn[i])
        y_rwkv = rwkv7_branch(seg_rwkv, rwkv_mu[i], rwkv_w0[i], rwkv_w2[i], rwkv_a0[i], rwkv_a2[i],
                              rwkv_k_k[i], rwkv_k_a[i], rwkv_r_k[i], rwkv_ln_w[i], rwkv_ln_b[i])
        y_gdn = gdn_branch(seg_gdn, gdn_conv[i], gdn_a_log[i], gdn_dt_bias[i], gdn_norm[i])
        y = jnp.concatenate([y_ret, y_rwkv, y_gdn], axis=-1).astype(x.dtype)
        x = x + jnp.matmul(y, w_out[i])
        ple = jnp.matmul(p[i], w_ple[i]).astype(jnp.float32)
        gate = jax.nn.sigmoid(jnp.matmul(rms_norm(x, ple_norm[i]), w_ple_gate[i]).astype(jnp.float32))
        x = x + (ple * gate).astype(x.dtype)
    return rms_norm(x, final_norm)
```

```python
import functools
import math

import jax
import jax.numpy as jnp
from jax import lax
from jax.experimental import pallas as pl
from jax.experimental.pallas import tpu as pltpu

F32 = jnp.float32
BF16 = jnp.bfloat16

LANES = 128
VMEM_LIMIT = 56 * 1024 * 1024

D_MODEL = 4096
PLE_DIM = 256
RMS_EPS = 1e-6

RET_HEAD_DIM = 128
RET_WIDTH = 1024
RET_HEADS = RET_WIDTH // RET_HEAD_DIM
RET_CHUNK = 128
ROPE_BASE = 10000.0
RET_GN_EPS = 1e-5

RWKV_HEAD_DIM = 64
RWKV_WIDTH = 1024
RWKV_HEADS = RWKV_WIDTH // RWKV_HEAD_DIM
RWKV_PAIRS = RWKV_HEADS // 2
RWKV_LORA = 64
RWKV_GN_EPS = 64e-5
RWKV_CHUNK = 64

GDN_HEAD_DIM = 128
GDN_WIDTH = 2048
GDN_HEADS = GDN_WIDTH // GDN_HEAD_DIM
GDN_CONV = 4
GDN_CHUNK = 64

COL_RET = 0
COL_RWKV = 4096
COL_GDN = 8192
COL_LORA = 16384
COL_AB = 16512
IN_PAD = 16640
IN_WIDTH = 16544


def _mm(a, b, dims, mode):
    dn = {"nn": (((1,), (0,)), ((), ())),
          "nt": (((1,), (1,)), ((), ())),
          "tn": (((0,), (0,)), ((), ()))}[dims]
    if mode == "bf16":
        return lax.dot_general(a.astype(BF16), b.astype(BF16), dn, preferred_element_type=F32)
    return lax.dot_general(a.astype(F32), b.astype(F32), dn, precision=lax.Precision.HIGHEST,
                           preferred_element_type=F32)


def _softplus(x):
    return jnp.maximum(x, 0.0) + jnp.log1p(jnp.exp(-jnp.abs(x)))


def _sigmoid(x):
    return 1.0 / (1.0 + jnp.exp(-x))


def _silu(x):
    return x * _sigmoid(x)


def _unit_lower_inverse(n_mat, mode):
    c = n_mat.shape[0]
    row = lax.broadcasted_iota(jnp.int32, (c, c), 0)
    col = lax.broadcasted_iota(jnp.int32, (c, c), 1)
    eye = jnp.where(row == col, 1.0, 0.0).astype(F32)
    inv = eye + n_mat
    power = n_mat
    steps = int(math.log2(c)) - 1
    for _ in range(steps):
        power = _mm(power, power, "nn", mode)
        inv = inv + _mm(inv, power, "nn", mode)
    return inv


def _rms_rows(x, w):
    return x * lax.rsqrt(jnp.mean(x * x, axis=-1, keepdims=True) + RMS_EPS) * w


def _in_proj_kernel(x_ref, nw_ref, w_ref, o_ref, h_ref):
    @pl.when(pl.program_id(1) == 0)
    def _():
        h_ref[...] = _rms_rows(x_ref[...], nw_ref[...]).astype(BF16)

    o_ref[...] = jnp.dot(h_ref[...], w_ref[...], preferred_element_type=F32)


def _in_proj(x2d, norm_w, w_bf16, *, tm, tn):
    m, d = x2d.shape
    n = w_bf16.shape[1]
    return pl.pallas_call(
        _in_proj_kernel,
        grid=(m // tm, n // tn),
        in_specs=[pl.BlockSpec((tm, d), lambda i, j: (i, 0)),
                  pl.BlockSpec((1, d), lambda i, j: (0, 0)),
                  pl.BlockSpec((d, tn), lambda i, j: (0, j))],
        out_specs=pl.BlockSpec((tm, tn), lambda i, j: (i, j)),
        out_shape=jax.ShapeDtypeStruct((m, n), F32),
        scratch_shapes=[pltpu.VMEM((tm, d), BF16)],
        compiler_params=pltpu.CompilerParams(
            dimension_semantics=("parallel", "arbitrary"), vmem_limit_bytes=VMEM_LIMIT),
        name="in_proj",
    )(x2d, norm_w.reshape(1, d), w_bf16)


def _ret_kernel(pos_ref, invf_ref, gn_ref, q_ref, k_ref, v_ref, z_ref, o_ref, state_ref, *, nchunk):
    c = RET_CHUNK
    dh = RET_HEAD_DIM

    @pl.when(pl.program_id(1) == 0)
    def _():
        state_ref[...] = jnp.zeros_like(state_ref)

    row = lax.broadcasted_iota(jnp.int32, (c, c), 0)
    col = lax.broadcasted_iota(jnp.int32, (c, c), 1)
    rel = (row - col).astype(F32)
    idx = lax.broadcasted_iota(jnp.int32, (c, 1), 0).astype(F32)
    first_half = col < dh // 2

    for ci in range(nchunk):
        rows = slice(ci * c, (ci + 1) * c)
        ang = pos_ref[rows, :].astype(F32) * invf_ref[...]
        cos = jnp.cos(ang)
        sin = jnp.sin(ang)
        sin_signed = jnp.where(first_half, -sin, sin)
        for h in range(RET_HEADS):
            cols = slice(h * dh, (h + 1) * dh)
            log_gamma = math.log1p(-(2.0 ** (-5.0 - h)))
            mask = jnp.where(rel >= 0, jnp.exp(jnp.maximum(rel, 0.0) * log_gamma), 0.0)
            xi = jnp.exp((idx + 1.0) * log_gamma)
            zeta = jnp.exp((c - 1.0 - idx) * log_gamma)
            chunk_decay = math.exp(c * log_gamma)
            q = q_ref[rows, cols]
            k = k_ref[rows, cols]
            v = v_ref[rows, cols]
            qr = q * cos + pltpu.roll(q, dh // 2, 1) * sin_signed
            kr = (k * cos + pltpu.roll(k, dh // 2, 1) * sin_signed) * (dh ** -0.5)
            state = state_ref[h]
            scores = _mm(qr, kr, "nt", "bf16") * mask
            inner = _mm(scores, v, "nn", "bf16")
            cross = _mm(qr * xi, state, "nn", "bf16")
            kv = _mm(kr * zeta, v, "tn", "bf16")
            state_ref[h] = state * chunk_decay + kv
            y = inner + cross
            yc = y - jnp.mean(y, axis=-1, keepdims=True)
            yn = yc * lax.rsqrt(jnp.mean(yc * yc, axis=-1, keepdims=True) + RET_GN_EPS)
            o_ref[rows, cols] = (yn * gn_ref[:, cols] * _silu(z_ref[rows, cols])).astype(o_ref.dtype)


def _retention(proj, pos2d, inv_freq, gn_w, *, batch, seq, tb):
    m = batch * seq
    nb = seq // tb
    w = RET_WIDTH
    base = COL_RET // w
    kern = functools.partial(_ret_kernel, nchunk=tb // RET_CHUNK)
    tok = lambda off: pl.BlockSpec((tb, w), lambda b, j, off=off: (b * nb + j, base + off))
    return pl.pallas_call(
        kern,
        grid=(batch, nb),
        in_specs=[pl.BlockSpec((tb, 1), lambda b, j: (b * nb + j, 0)),
                  pl.BlockSpec((1, RET_HEAD_DIM), lambda b, j: (0, 0)),
                  pl.BlockSpec((1, w), lambda b, j: (0, 0)),
                  tok(0), tok(1), tok(2), tok(3)],
        out_specs=pl.BlockSpec((tb, w), lambda b, j: (b * nb + j, 0)),
        out_shape=jax.ShapeDtypeStruct((m, w), BF16),
        scratch_shapes=[pltpu.VMEM((RET_HEADS, RET_HEAD_DIM, RET_HEAD_DIM), F32)],
        compiler_params=pltpu.CompilerParams(
            dimension_semantics=("parallel", "arbitrary"), vmem_limit_bytes=VMEM_LIMIT),
        name="retention",
    )(pos2d, inv_freq, gn_w.reshape(1, w), proj, proj, proj, proj)


_P_MU_R, _P_MU_K, _P_MU_V, _P_MU_Z, _P_W0, _P_A0, _P_KK, _P_KA, _P_RK, _P_LNW, _P_LNB = range(11)
_P_ROWS = 16


def _shift_rows(x, prev_row):
    rolled = pltpu.roll(x, 1, 0)
    first = lax.broadcasted_iota(jnp.int32, x.shape, 0) == 0
    return jnp.where(first, prev_row, rolled)


def _rwkv_kernel(par_ref, mul_ref, w2_ref, a2_ref, r_ref, k_ref, v_ref, z_ref, lo_ref, o_ref,
                 state_ref, prev_ref, prevlo_ref, *, nchunk):
    c = RWKV_CHUNK
    hd = RWKV_HEAD_DIM
    pw = 2 * hd

    @pl.when(pl.program_id(1) == 0)
    def _():
        state_ref[...] = jnp.zeros_like(state_ref)
        prev_ref[...] = jnp.zeros_like(prev_ref)
        prevlo_ref[...] = jnp.zeros_like(prevlo_ref)

    row = lax.broadcasted_iota(jnp.int32, (c, c), 0)
    col = lax.broadcasted_iota(jnp.int32, (c, c), 1)
    tril_incl = jnp.where(row >= col, 1.0, 0.0).astype(F32)
    lane = lax.broadcasted_iota(jnp.int32, (1, pw), 1)
    head_mask = [jnp.where(lane < hd, 1.0, 0.0).astype(F32), jnp.where(lane >= hd, 1.0, 0.0).astype(F32)]
    lane_c = lax.broadcasted_iota(jnp.int32, (c, pw), 1)
    row_c = lax.broadcasted_iota(jnp.int32, (c, pw), 0)
    in_head0 = lane_c < hd
    col_in_half = jnp.where(in_head0, lane_c, lane_c - hd)
    strict_lo = row_c > col_in_half
    incl_lo = row_c >= col_in_half
    r2 = lax.broadcasted_iota(jnp.int32, (pw, pw), 0)
    c2 = lax.broadcasted_iota(jnp.int32, (pw, pw), 1)
    same_head = jnp.where((r2 < hd) == (c2 < hd), 1.0, 0.0).astype(F32)

    for ci in range(nchunk):
        rows = slice(ci * c, (ci + 1) * c)
        lo_raw = lo_ref[rows, :]
        lo = lo_raw + (_shift_rows(lo_raw, prevlo_ref[0:1, :]) - lo_raw) * mul_ref[...]
        prevlo_ref[0:1, :] = lo_raw[c - 1:c, :]
        dw = jnp.tanh(lo[:, :RWKV_LORA])
        da = lo[:, RWKV_LORA:]
        for p in range(RWKV_PAIRS):
            cols = slice(p * pw, (p + 1) * pw)
            par = lambda i: par_ref[i:i + 1, cols]

            def mixed(ref, slot, mu_row):
                raw = ref[rows, cols]
                out = raw + (_shift_rows(raw, prev_ref[slot:slot + 1, cols]) - raw) * par(mu_row)
                prev_ref[slot:slot + 1, cols] = raw[c - 1:c, :]
                return out

            r = mixed(r_ref, 0, _P_MU_R)
            k = mixed(k_ref, 1, _P_MU_K)
            v = mixed(v_ref, 2, _P_MU_V)
            z = mixed(z_ref, 3, _P_MU_Z)

            w_log = -_softplus(-(par(_P_W0) + _mm(dw, w2_ref[:, cols], "nn", "f32"))) - 0.5
            logw = -jnp.exp(w_log)
            a = _sigmoid(par(_P_A0) + _mm(da, a2_ref[:, cols], "nn", "f32"))
            kk_raw = k * par(_P_KK)
            kk = kk_raw * lax.rsqrt(_mm(kk_raw * kk_raw, same_head, "nn", "f32") + 1e-6)
            k = k * (1.0 + (a - 1.0) * par(_P_KA))

            cs = _mm(tril_incl, logw, "nn", "f32")
            tot = cs[c - 1:c, :]
            e_cs = jnp.exp(cs)
            e_prev = jnp.exp(cs - logw)
            e_neg = jnp.exp(-cs)
            e_tail = jnp.exp(tot - cs)
            neg_kka = -(kk * a)
            rt = r * e_cs
            bt = kk * e_prev
            ak = jnp.concatenate([neg_kka * e_neg, k * e_neg], axis=0)
            ak_tail = jnp.concatenate([neg_kka * e_tail, k * e_tail], axis=0)
            st = state_ref[p]
            from_state = _mm(jnp.concatenate([bt, rt], axis=0), st, "nt", "f32")
            xa = from_state[:c]
            ya = from_state[c:]
            vv = jnp.concatenate([v, v], axis=0)
            us = []
            gbs = []
            for hh in range(2):
                br = jnp.concatenate([bt * head_mask[hh], rt * head_mask[hh]], axis=0)
                g = _mm(br, ak, "nt", "f32")
                gt = jnp.where(strict_lo, g[:c], 0.0)
                gbs.append(jnp.where(incl_lo, g[c:], 0.0))
                t_inv = _unit_lower_inverse(gt[:, :c], "f32")
                x0 = _mm(jnp.where(in_head0, 0.0, gt), vv, "nn", "f32")
                us.append(_mm(t_inv, xa + x0, "nn", "f32"))
            u = jnp.where(in_head0, us[0], us[1])
            uv = jnp.concatenate([u, v], axis=0)
            y = ya + jnp.where(in_head0, _mm(gbs[0], uv, "nn", "f32"), _mm(gbs[1], uv, "nn", "f32"))
            state_ref[p] = st * jnp.exp(tot) + same_head * _mm(uv, ak_tail, "tn", "f32")

            inv_n = 1.0 / hd
            mean = _mm(y, same_head, "nn", "f32") * inv_n
            yc = y - mean
            var = _mm(yc * yc, same_head, "nn", "f32") * inv_n
            yn = yc * lax.rsqrt(var + RWKV_GN_EPS) * par(_P_LNW) + par(_P_LNB)
            bonus = _mm(r * k * par(_P_RK), same_head, "nn", "f32") * v
            o_ref[rows, cols] = ((yn + bonus) * _silu(z)).astype(o_ref.dtype)


def _rwkv(proj, params, mu_lora, w2, a2, *, batch, seq, tb):
    m = batch * seq
    nb = seq // tb
    w = RWKV_WIDTH
    base = COL_RWKV // w
    kern = functools.partial(_rwkv_kernel, nchunk=tb // RWKV_CHUNK)
    tok = lambda off: pl.BlockSpec((tb, w), lambda b, j, off=off: (b * nb + j, base + off))
    const = lambda shape: pl.BlockSpec(shape, lambda b, j: (0, 0))
    return pl.pallas_call(
        kern,
        grid=(batch, nb),
        in_specs=[const((_P_ROWS, w)), const((1, LANES)), const((RWKV_LORA, w)), const((RWKV_LORA, w)),
                  tok(0), tok(1), tok(2), tok(3),
                  pl.BlockSpec((tb, LANES), lambda b, j: (b * nb + j, COL_LORA // LANES))],
        out_specs=pl.BlockSpec((tb, w), lambda b, j: (b * nb + j, 0)),
        out_shape=jax.ShapeDtypeStruct((m, w), BF16),
        scratch_shapes=[pltpu.VMEM((RWKV_PAIRS, 2 * RWKV_HEAD_DIM, 2 * RWKV_HEAD_DIM), F32),
                        pltpu.VMEM((8, w), F32),
                        pltpu.VMEM((8, LANES), F32)],
        compiler_params=pltpu.CompilerParams(
            dimension_semantics=("parallel", "arbitrary"), vmem_limit_bytes=VMEM_LIMIT),
        name="rwkv7",
    )(params, mu_lora, w2, a2, proj, proj, proj, proj, proj)


def _gdn_kernel(conv_ref, alog_ref, dt_ref, nw_ref, q_ref, k_ref, v_ref, z_ref, ab_ref, o_ref,
                state_ref, prev_ref, *, nchunk):
    c = GDN_CHUNK
    dh = GDN_HEAD_DIM
    gw = GDN_WIDTH

    @pl.when(pl.program_id(1) == 0)
    def _():
        state_ref[...] = jnp.zeros_like(state_ref)
        prev_ref[...] = jnp.zeros_like(prev_ref)

    row = lax.broadcasted_iota(jnp.int32, (c, c), 0)
    col = lax.broadcasted_iota(jnp.int32, (c, c), 1)
    causal = row >= col
    strict = row > col
    tril_incl = jnp.where(causal, 1.0, 0.0).astype(F32)

    def conv_silu(ref, slot, rows, cols):
        cur = ref[rows, cols]
        ext = jnp.concatenate([prev_ref[slot, :, cols], cur], axis=0)
        wcols = slice(slot * gw + cols.start, slot * gw + cols.stop)
        acc = cur * conv_ref[GDN_CONV - 1:GDN_CONV, wcols]
        for s in range(1, GDN_CONV):
            shifted = pltpu.roll(ext, s, 0)[8:, :]
            acc = acc + shifted * conv_ref[GDN_CONV - 1 - s:GDN_CONV - s, wcols]
        prev_ref[slot, :, cols] = cur[c - 8:, :]
        return _silu(acc)

    for ci in range(nchunk):
        rows = slice(ci * c, (ci + 1) * c)
        ab = ab_ref[rows, :]
        g_all = -jnp.exp(alog_ref[...]) * _softplus(ab + dt_ref[...])
        beta_all = _sigmoid(ab)
        gc_all = _mm(tril_incl, g_all, "nn", "f32")
        gc_rows = gc_all.T
        for h in range(GDN_HEADS):
            cols = slice(h * dh, (h + 1) * dh)
            gcol = gc_all[:, h:h + 1]
            grow = gc_rows[h:h + 1, :]
            beta = beta_all[:, GDN_HEADS + h:GDN_HEADS + h + 1]
            decay = jnp.where(causal, jnp.exp(jnp.minimum(gcol - grow, 0.0)), 0.0)
            q = conv_silu(q_ref, 0, rows, cols)
            k = conv_silu(k_ref, 1, rows, cols)
            v = conv_silu(v_ref, 2, rows, cols)
            q = q * lax.rsqrt(jnp.sum(q * q, axis=-1, keepdims=True) + 1e-6) * (dh ** -0.5)
            k = k * lax.rsqrt(jnp.sum(k * k, axis=-1, keepdims=True) + 1e-6)
            kb = k * beta
            eg = jnp.exp(gcol)
            glast = gcol[c - 1:c, :]
            lower = jnp.where(strict, _mm(kb, k, "nt", "f32") * decay, 0.0)
            t_inv = _unit_lower_inverse(-lower, "f32")
            sol = _mm(t_inv, jnp.concatenate([v * beta, kb * eg], axis=1), "nn", "f32")
            u = sol[:, :dh]
            wk = sol[:, dh:]
            attn = _mm(q, k, "nt", "f32") * decay
            s = state_ref[h]
            v_new = u - _mm(wk, s, "nn", "f32")
            o = _mm(q * eg, s, "nn", "f32") + _mm(attn, v_new, "nn", "f32")
            state_ref[h] = s * jnp.exp(glast) + _mm(k * jnp.exp(glast - gcol), v_new, "tn", "f32")
            o = o * lax.rsqrt(jnp.mean(o * o, axis=-1, keepdims=True) + RMS_EPS) * nw_ref[...]
            o_ref[rows, cols] = (o * _silu(z_ref[rows, cols])).astype(o_ref.dtype)


def _gdn(proj, conv_w, alog_pad, dt_pad, norm_w, *, batch, seq, tb):
    m = batch * seq
    nb = seq // tb
    w = GDN_WIDTH
    base = COL_GDN // w
    kern = functools.partial(_gdn_kernel, nchunk=tb // GDN_CHUNK)
    tok = lambda off: pl.BlockSpec((tb, w), lambda b, j, off=off: (b * nb + j, base + off))
    const = lambda shape: pl.BlockSpec(shape, lambda b, j: (0, 0))
    return pl.pallas_call(
        kern,
        grid=(batch, nb),
        in_specs=[const((GDN_CONV, 3 * w)), const((1, LANES)), const((1, LANES)), const((1, GDN_HEAD_DIM)),
                  tok(0), tok(1), tok(2), tok(3),
                  pl.BlockSpec((tb, LANES), lambda b, j: (b * nb + j, COL_AB // LANES))],
        out_specs=pl.BlockSpec((tb, w), lambda b, j: (b * nb + j, 0)),
        out_shape=jax.ShapeDtypeStruct((m, w), BF16),
        scratch_shapes=[pltpu.VMEM((GDN_HEADS, GDN_HEAD_DIM, GDN_HEAD_DIM), F32),
                        pltpu.VMEM((3, 8, w), F32)],
        compiler_params=pltpu.CompilerParams(
            dimension_semantics=("parallel", "arbitrary"), vmem_limit_bytes=VMEM_LIMIT),
        name="gdn",
    )(conv_w, alog_pad, dt_pad, norm_w.reshape(1, GDN_HEAD_DIM), proj, proj, proj, proj, proj)


def _out_proj_kernel(x_ref, y1_ref, y2_ref, y3_ref, w1_ref, w2_ref, w3_ref, o_ref):
    acc = jnp.dot(y1_ref[...], w1_ref[...], preferred_element_type=F32)
    acc = acc + jnp.dot(y2_ref[...], w2_ref[...], preferred_element_type=F32)
    acc = acc + jnp.dot(y3_ref[...], w3_ref[...], preferred_element_type=F32)
    o_ref[...] = x_ref[...] + acc


def _out_proj(x2d, y_ret, y_rwkv, y_gdn, w_bf16, *, tm, tn):
    m, d = x2d.shape
    k1, k2, k3 = y_ret.shape[1], y_rwkv.shape[1], y_gdn.shape[1]
    return pl.pallas_call(
        _out_proj_kernel,
        grid=(m // tm, d // tn),
        in_specs=[pl.BlockSpec((tm, tn), lambda i, j: (i, j)),
                  pl.BlockSpec((tm, k1), lambda i, j: (i, 0)),
                  pl.BlockSpec((tm, k2), lambda i, j: (i, 0)),
                  pl.BlockSpec((tm, k3), lambda i, j: (i, 0)),
                  pl.BlockSpec((k1, tn), lambda i, j: (0, j)),
                  pl.BlockSpec((k2, tn), lambda i, j: (1, j)),
                  pl.BlockSpec((k3, tn), lambda i, j: (1, j))],
        out_specs=pl.BlockSpec((tm, tn), lambda i, j: (i, j)),
        out_shape=jax.ShapeDtypeStruct((m, d), F32),
        compiler_params=pltpu.CompilerParams(
            dimension_semantics=("parallel", "arbitrary"), vmem_limit_bytes=VMEM_LIMIT),
        name="out_proj",
    )(x2d, y_ret, y_rwkv, y_gdn, w_bf16, w_bf16, w_bf16)


def _ple_kernel(x_ref, nw_ref, p_ref, wp_ref, wg_ref, o_ref, h_ref, *, tn):
    j = pl.program_id(1)

    @pl.when(j == 0)
    def _():
        h_ref[...] = _rms_rows(x_ref[...], nw_ref[...]).astype(BF16)

    gate = _sigmoid(jnp.dot(h_ref[...], wg_ref[...], preferred_element_type=F32))
    ple = jnp.dot(p_ref[...].astype(BF16), wp_ref[...], preferred_element_type=F32)
    cols = pl.ds(pl.multiple_of(j * tn, tn), tn)
    o_ref[...] = x_ref[:, cols] + ple * gate


def _ple(x2d, norm_w, p2d, w_ple_bf16, w_gate_bf16, *, tm, tn):
    m, d = x2d.shape
    pd = p2d.shape[1]
    kern = functools.partial(_ple_kernel, tn=tn)
    return pl.pallas_call(
        kern,
        grid=(m // tm, d // tn),
        in_specs=[pl.BlockSpec((tm, d), lambda i, j: (i, 0)),
                  pl.BlockSpec((1, d), lambda i, j: (0, 0)),
                  pl.BlockSpec((tm, pd), lambda i, j: (i, 0)),
                  pl.BlockSpec((pd, tn), lambda i, j: (0, j)),
                  pl.BlockSpec((d, tn), lambda i, j: (0, j))],
        out_specs=pl.BlockSpec((tm, tn), lambda i, j: (i, j)),
        out_shape=jax.ShapeDtypeStruct((m, d), F32),
        scratch_shapes=[pltpu.VMEM((tm, d), BF16)],
        compiler_params=pltpu.CompilerParams(
            dimension_semantics=("parallel", "arbitrary"), vmem_limit_bytes=VMEM_LIMIT),
        name="ple_gate",
    )(x2d, norm_w.reshape(1, d), p2d, w_ple_bf16, w_gate_bf16)


def _final_norm_kernel(x_ref, w_ref, o_ref):
    o_ref[...] = _rms_rows(x_ref[...], w_ref[...])


def _final_norm(x2d, w, *, tm):
    m, d = x2d.shape
    return pl.pallas_call(
        _final_norm_kernel,
        grid=(m // tm,),
        in_specs=[pl.BlockSpec((tm, d), lambda i: (i, 0)), pl.BlockSpec((1, d), lambda i: (0, 0))],
        out_specs=pl.BlockSpec((tm, d), lambda i: (i, 0)),
        out_shape=jax.ShapeDtypeStruct((m, d), F32),
        compiler_params=pltpu.CompilerParams(
            dimension_semantics=("parallel",), vmem_limit_bytes=VMEM_LIMIT),
        name="final_norm",
    )(x2d, w.reshape(1, d))


def _reorder_in_weight(w):
    rwkv_lora0 = 4096 + 4 * RWKV_WIDTH
    gdn0 = rwkv_lora0 + 2 * RWKV_LORA
    ab0 = gdn0 + 4 * GDN_WIDTH
    pad = jnp.zeros((w.shape[0], IN_PAD - IN_WIDTH), w.dtype)
    out = jnp.concatenate([w[:, :rwkv_lora0], w[:, gdn0:ab0], w[:, rwkv_lora0:gdn0], w[:, ab0:], pad], axis=1)
    return out.astype(BF16)


def _pad_lanes(v):
    return jnp.pad(v.astype(F32), (0, LANES - v.shape[0])).reshape(1, LANES)


def kernel(x, p, positions, norm_w, w_in, ret_gn, rwkv_mu, rwkv_w0, rwkv_w2, rwkv_a0, rwkv_a2, rwkv_k_k, rwkv_k_a, rwkv_r_k, rwkv_ln_w, rwkv_ln_b, gdn_conv, gdn_a_log, gdn_dt_bias, gdn_norm, w_out, w_ple, ple_norm, w_ple_gate, final_norm):
    batch, seq, d = x.shape
    depth = w_in.shape[0]
    m = batch * seq
    tm = min(512, m)
    tb_ret = min(512, seq)
    tb_rwkv = min(64, seq)
    tb_gdn = min(64, seq)

    half = RET_HEAD_DIM // 2
    inv_freq = ROPE_BASE ** (-jnp.arange(half, dtype=F32) / half)
    inv_freq = jnp.concatenate([inv_freq, inv_freq]).reshape(1, RET_HEAD_DIM)
    pos2d = positions.reshape(m, 1)

    xc = x.reshape(m, d)
    for i in range(depth):
        proj = _in_proj(xc, norm_w[i], _reorder_in_weight(w_in[i]), tm=tm, tn=640)
        y_ret = _retention(proj, pos2d, inv_freq, ret_gn[i], batch=batch, seq=seq, tb=tb_ret)

        mu = rwkv_mu[i]
        wd = RWKV_WIDTH
        rows = [mu[0:wd], mu[wd:2 * wd], mu[2 * wd:3 * wd], mu[3 * wd:4 * wd], rwkv_w0[i], rwkv_a0[i],
                rwkv_k_k[i], rwkv_k_a[i], rwkv_r_k[i], rwkv_ln_w[i], rwkv_ln_b[i]]
        params = jnp.concatenate([jnp.stack(rows), jnp.zeros((_P_ROWS - len(rows), wd), F32)], axis=0)
        mu_lora = mu[4 * wd:].reshape(1, LANES)
        y_rwkv = _rwkv(proj, params, mu_lora, rwkv_w2[i], rwkv_a2[i], batch=batch, seq=seq, tb=tb_rwkv)

        alog_pad = _pad_lanes(gdn_a_log[i])
        dt_pad = _pad_lanes(gdn_dt_bias[i])
        y_gdn = _gdn(proj, gdn_conv[i].astype(F32), alog_pad, dt_pad, gdn_norm[i],
                     batch=batch, seq=seq, tb=tb_gdn)

        xc = _out_proj(xc, y_ret, y_rwkv, y_gdn, w_out[i].astype(BF16), tm=tm, tn=1024)
        xc = _ple(xc, ple_norm[i], p[i].reshape(m, PLE_DIM), w_ple[i].astype(BF16),
                  w_ple_gate[i].astype(BF16), tm=tm, tn=512)
    return _final_norm(xc, final_norm, tm=tm).reshape(batch, seq, d)
```

```python
import functools
import math

import jax
import jax.numpy as jnp
from jax import lax
from jax.experimental import pallas as pl
from jax.experimental.pallas import tpu as pltpu

F32 = jnp.float32
BF16 = jnp.bfloat16

LANES = 128
VMEM_LIMIT = 56 * 1024 * 1024

D_MODEL = 4096
PLE_DIM = 256
RMS_EPS = 1e-6

RET_HEAD_DIM = 128
RET_WIDTH = 1024
RET_HEADS = RET_WIDTH // RET_HEAD_DIM
RET_CHUNK = 128
ROPE_BASE = 10000.0
RET_GN_EPS = 1e-5

RWKV_HEAD_DIM = 64
RWKV_WIDTH = 1024
RWKV_HEADS = RWKV_WIDTH // RWKV_HEAD_DIM
RWKV_PAIRS = RWKV_HEADS // 2
RWKV_LORA = 64
RWKV_GN_EPS = 64e-5
RWKV_CHUNK = 64

GDN_HEAD_DIM = 128
GDN_WIDTH = 2048
GDN_HEADS = GDN_WIDTH // GDN_HEAD_DIM
GDN_CONV = 4
GDN_CHUNK = 64

COL_RET = 0
COL_RWKV = 4096
COL_GDN = 8192
COL_LORA = 16384
COL_AB = 16512
IN_PAD = 16640
IN_WIDTH = 16544


def _mm(a, b, dims, mode):
    dn = {"nn": (((1,), (0,)), ((), ())),
          "nt": (((1,), (1,)), ((), ())),
          "tn": (((0,), (0,)), ((), ()))}[dims]
    dot = functools.partial(lax.dot_general, dimension_numbers=dn, preferred_element_type=F32)
    if mode == "bf16":
        return dot(a.astype(BF16), b.astype(BF16))
    if mode == "x3":
        a_hi, b_hi = a.astype(BF16), b.astype(BF16)
        a_lo = (a - a_hi.astype(F32)).astype(BF16)
        b_lo = (b - b_hi.astype(F32)).astype(BF16)
        return dot(a_hi, b_hi) + (dot(a_hi, b_lo) + dot(a_lo, b_hi))
    return dot(a.astype(F32), b.astype(F32), precision=lax.Precision.HIGHEST)


def _mm_exact_lhs(a_exact, b):
    b1 = b.astype(BF16)
    r1 = b - b1.astype(F32)
    b2 = r1.astype(BF16)
    b3 = (r1 - b2.astype(F32)).astype(BF16)
    a16 = a_exact.astype(BF16)
    return _mm(a16, b1, "nn", "bf16") + (_mm(a16, b2, "nn", "bf16") + _mm(a16, b3, "nn", "bf16"))


def _softplus(x):
    return jnp.maximum(x, 0.0) + jnp.log1p(jnp.exp(-jnp.abs(x)))


def _sigmoid(x):
    return 1.0 / (1.0 + jnp.exp(-x))


def _silu(x):
    return x * _sigmoid(x)


def _block_unit_lower_inverses(n_mats, nilpotent_order, mode):
    n = n_mats[0].shape[0]
    row = lax.broadcasted_iota(jnp.int32, (n, n), 0)
    col = lax.broadcasted_iota(jnp.int32, (n, n), 1)
    eye = jnp.where(row == col, 1.0, 0.0).astype(F32)
    accs = [eye + m for m in n_mats]
    powers = [_mm(m, m, "nn", mode) for m in n_mats]
    steps = int(math.log2(nilpotent_order)) - 1
    for j in range(steps):
        if j == steps - 1:
            accs = [acc + _mm(acc, pw_, "nn", mode) for acc, pw_ in zip(accs, powers)]
        else:
            both = [_mm(jnp.concatenate([pw_, acc], axis=0), pw_, "nn", mode) for acc, pw_ in zip(accs, powers)]
            powers = [b[:n] for b in both]
            accs = [acc + b[n:] for acc, b in zip(accs, both)]
    return accs


def _rms_rows(x, w):
    return x * lax.rsqrt(jnp.mean(x * x, axis=-1, keepdims=True) + RMS_EPS) * w


def _in_proj_kernel(x_ref, nw_ref, w_ref, o_ref, h_ref):
    @pl.when(pl.program_id(1) == 0)
    def _():
        h_ref[...] = _rms_rows(x_ref[...], nw_ref[...]).astype(BF16)

    o_ref[...] = jnp.dot(h_ref[...], w_ref[...], preferred_element_type=F32)


def _in_proj(x2d, norm_w, w_bf16, *, tm, tn):
    m, d = x2d.shape
    n = w_bf16.shape[1]
    return pl.pallas_call(
        _in_proj_kernel,
        grid=(m // tm, n // tn),
        in_specs=[pl.BlockSpec((tm, d), lambda i, j: (i, 0)),
                  pl.BlockSpec((1, d), lambda i, j: (0, 0)),
                  pl.BlockSpec((d, tn), lambda i, j: (0, j))],
        out_specs=pl.BlockSpec((tm, tn), lambda i, j: (i, j)),
        out_shape=jax.ShapeDtypeStruct((m, n), F32),
        scratch_shapes=[pltpu.VMEM((tm, d), BF16)],
        compiler_params=pltpu.CompilerParams(
            dimension_semantics=("parallel", "arbitrary"), vmem_limit_bytes=VMEM_LIMIT),
        name="in_proj",
    )(x2d, norm_w.reshape(1, d), w_bf16)


def _ret_kernel(pos_ref, invf_ref, gn_ref, q_ref, k_ref, v_ref, z_ref, o_ref, state_ref, *, nchunk):
    c = RET_CHUNK
    dh = RET_HEAD_DIM

    @pl.when(pl.program_id(1) == 0)
    def _():
        state_ref[...] = jnp.zeros_like(state_ref)

    row = lax.broadcasted_iota(jnp.int32, (c, c), 0)
    col = lax.broadcasted_iota(jnp.int32, (c, c), 1)
    rel = (row - col).astype(F32)
    idx = lax.broadcasted_iota(jnp.int32, (c, 1), 0).astype(F32)
    first_half = col < dh // 2

    for ci in range(nchunk):
        rows = slice(ci * c, (ci + 1) * c)
        ang = pos_ref[rows, :].astype(F32) * invf_ref[...]
        cos = jnp.cos(ang)
        sin = jnp.sin(ang)
        sin_signed = jnp.where(first_half, -sin, sin)
        for h in range(RET_HEADS):
            cols = slice(h * dh, (h + 1) * dh)
            log_gamma = math.log1p(-(2.0 ** (-5.0 - h)))
            mask = jnp.where(rel >= 0, jnp.exp(jnp.maximum(rel, 0.0) * log_gamma), 0.0)
            xi = jnp.exp((idx + 1.0) * log_gamma)
            zeta = jnp.exp((c - 1.0 - idx) * log_gamma)
            chunk_decay = math.exp(c * log_gamma)
            q = q_ref[rows, cols]
            k = k_ref[rows, cols]
            v = v_ref[rows, cols]
            qr = q * cos + pltpu.roll(q, dh // 2, 1) * sin_signed
            kr = (k * cos + pltpu.roll(k, dh // 2, 1) * sin_signed) * (dh ** -0.5)
            state = state_ref[h]
            scores = _mm(qr, kr, "nt", "bf16") * mask
            inner = _mm(scores, v, "nn", "bf16")
            cross = _mm(qr * xi, state, "nn", "bf16")
            kv = _mm(kr * zeta, v, "tn", "bf16")
            state_ref[h] = state * chunk_decay + kv
            y = inner + cross
            yc = y - jnp.mean(y, axis=-1, keepdims=True)
            yn = yc * lax.rsqrt(jnp.mean(yc * yc, axis=-1, keepdims=True) + RET_GN_EPS)
            o_ref[rows, cols] = (yn * gn_ref[:, cols] * _silu(z_ref[rows, cols])).astype(o_ref.dtype)


def _retention(proj, pos2d, inv_freq, gn_w, *, batch, seq, tb):
    m = batch * seq
    nb = seq // tb
    w = RET_WIDTH
    base = COL_RET // w
    kern = functools.partial(_ret_kernel, nchunk=tb // RET_CHUNK)
    tok = lambda off: pl.BlockSpec((tb, w), lambda b, j, off=off: (b * nb + j, base + off))
    return pl.pallas_call(
        kern,
        grid=(batch, nb),
        in_specs=[pl.BlockSpec((tb, 1), lambda b, j: (b * nb + j, 0)),
                  pl.BlockSpec((1, RET_HEAD_DIM), lambda b, j: (0, 0)),
                  pl.BlockSpec((1, w), lambda b, j: (0, 0)),
                  tok(0), tok(1), tok(2), tok(3)],
        out_specs=pl.BlockSpec((tb, w), lambda b, j: (b * nb + j, 0)),
        out_shape=jax.ShapeDtypeStruct((m, w), BF16),
        scratch_shapes=[pltpu.VMEM((RET_HEADS, RET_HEAD_DIM, RET_HEAD_DIM), F32)],
        compiler_params=pltpu.CompilerParams(
            dimension_semantics=("parallel", "arbitrary"), vmem_limit_bytes=VMEM_LIMIT),
        name="retention",
    )(pos2d, inv_freq, gn_w.reshape(1, w), proj, proj, proj, proj)


_P_MU_R, _P_MU_K, _P_MU_V, _P_MU_Z, _P_W0, _P_A0, _P_KK, _P_KA, _P_RK, _P_LNW, _P_LNB = range(11)
_P_ROWS = 16

RWKV_MODE = {"lora": "bf16", "gram": "bf16", "inv": "bf16", "apply": "bf16", "state": "bf16", "norm": "bf16"}


def _shift_rows(x, prev_row):
    rolled = pltpu.roll(x, 1, 0)
    first = lax.broadcasted_iota(jnp.int32, x.shape, 0) == 0
    return jnp.where(first, prev_row, rolled)


def _rwkv_kernel(par_ref, mul_ref, w2_ref, a2_ref, r_ref, k_ref, v_ref, z_ref, lo_ref, o_ref,
                 state_ref, prev_ref, prevlo_ref, *, nchunk):
    c = RWKV_CHUNK
    hd = RWKV_HEAD_DIM
    pw = 2 * hd
    md = RWKV_MODE

    @pl.when(pl.program_id(1) == 0)
    def _():
        state_ref[...] = jnp.zeros_like(state_ref)
        prev_ref[...] = jnp.zeros_like(prev_ref)
        prevlo_ref[...] = jnp.zeros_like(prevlo_ref)

    row = lax.broadcasted_iota(jnp.int32, (c, c), 0)
    col = lax.broadcasted_iota(jnp.int32, (c, c), 1)
    tril_incl = jnp.where(row >= col, 1.0, 0.0).astype(F32)
    lane = lax.broadcasted_iota(jnp.int32, (1, pw), 1)
    m0 = jnp.where(lane < hd, 1.0, 0.0).astype(F32)
    m1 = 1.0 - m0
    in_head0 = lax.broadcasted_iota(jnp.int32, (c, pw), 1) < hd
    r2 = lax.broadcasted_iota(jnp.int32, (pw, pw), 0)
    c2 = lax.broadcasted_iota(jnp.int32, (pw, pw), 1)
    same_blk = (r2 < hd) == (c2 < hd)
    same_head = jnp.where(same_blk, 1.0, 0.0).astype(F32)
    strict_bd = jnp.logical_and(same_blk, r2 > c2)
    incl_bd = jnp.logical_and(same_blk, r2 >= c2)
    incl_bd2 = jnp.concatenate([incl_bd, incl_bd], axis=1)

    def stack_heads(x):
        return jnp.concatenate([x * m0, x * m1], axis=0)

    def unstack_heads(x_st):
        return jnp.where(in_head0, x_st[:c], x_st[c:])

    def mixed(raw, prev_row, mu):
        return raw + (_shift_rows(raw, prev_row) - raw) * mu

    for ci in range(nchunk):
        rows = slice(ci * c, (ci + 1) * c)
        lo_raw = lo_ref[rows, :]
        lo = mixed(lo_raw, prevlo_ref[0:1, :], mul_ref[...])
        prevlo_ref[0:1, :] = lo_raw[c - 1:c, :]
        r_raw, k_raw, v_raw, z_raw = r_ref[rows, :], k_ref[rows, :], v_ref[rows, :], z_ref[rows, :]
        r_all = mixed(r_raw, prev_ref[0:1, :], par_ref[_P_MU_R:_P_MU_R + 1, :])
        k_all = mixed(k_raw, prev_ref[1:2, :], par_ref[_P_MU_K:_P_MU_K + 1, :])
        v_all = mixed(v_raw, prev_ref[2:3, :], par_ref[_P_MU_V:_P_MU_V + 1, :])
        z_all = mixed(z_raw, prev_ref[3:4, :], par_ref[_P_MU_Z:_P_MU_Z + 1, :])
        prev_ref[0:1, :] = r_raw[c - 1:c, :]
        prev_ref[1:2, :] = k_raw[c - 1:c, :]
        prev_ref[2:3, :] = v_raw[c - 1:c, :]
        prev_ref[3:4, :] = z_raw[c - 1:c, :]

        dw = jnp.tanh(lo[:, :RWKV_LORA])
        da = lo[:, RWKV_LORA:]
        w_log = -_softplus(-(par_ref[_P_W0:_P_W0 + 1, :] + _mm(dw, w2_ref[...], "nn", md["lora"]))) - 0.5
        logw_all = -jnp.exp(w_log)
        a_all = _sigmoid(par_ref[_P_A0:_P_A0 + 1, :] + _mm(da, a2_ref[...], "nn", md["lora"]))
        cs_all = _mm_exact_lhs(tril_incl, logw_all)

        pairs = range(RWKV_PAIRS)
        pcols = [slice(p * pw, (p + 1) * pw) for p in pairs]
        par = lambda i, p: par_ref[i:i + 1, pcols[p]]
        kk_raw = [k_all[:, pcols[p]] * par(_P_KK, p) for p in pairs]
        kk_ss = [_mm(x * x, same_head, "nn", md["norm"]) for x in kk_raw]
        br_st, ak_st, ak_tail, vs, ks, e_tot = [], [], [], [], [], []
        for p in pairs:
            cols = pcols[p]
            r, v, a = r_all[:, cols], v_all[:, cols], a_all[:, cols]
            logw, cs = logw_all[:, cols], cs_all[:, cols]
            kk = kk_raw[p] * lax.rsqrt(kk_ss[p] + 1e-6)
            k = k_all[:, cols] * (1.0 + (a - 1.0) * par(_P_KA, p))
            tot = cs[c - 1:c, :]
            e_neg = jnp.exp(-cs)
            e_tail = jnp.exp(tot - cs)
            neg_kka = -(kk * a)
            rt = r * jnp.exp(cs)
            bt = kk * jnp.exp(cs - logw)
            br_st.append(jnp.concatenate([stack_heads(bt), stack_heads(rt)], axis=0))
            ak_st.append(jnp.concatenate([stack_heads(neg_kka * e_neg), stack_heads(k * e_neg)], axis=0))
            ak_tail.append(jnp.concatenate([neg_kka * e_tail, k * e_tail], axis=0))
            vs.append(v)
            ks.append(k)
            e_tot.append(jnp.exp(tot))
        grams = [_mm(br_st[p], ak_st[p], "nt", md["gram"]) for p in pairs]
        t_bd = _block_unit_lower_inverses([jnp.where(strict_bd, g[:pw, :pw], 0.0) for g in grams], c, md["inv"])
        vvs = [jnp.concatenate([v, v], axis=0) for v in vs]
        x0 = [_mm(jnp.where(strict_bd, grams[p][:pw, pw:], 0.0), vvs[p], "nn", md["apply"]) for p in pairs]
        from_state = [_mm(br_st[p], state_ref[p], "nt", md["state"]) for p in pairs]
        u_st = [_mm(t_bd[p], from_state[p][:pw] + x0[p], "nn", md["apply"]) for p in pairs]
        y_st = [from_state[p][pw:]
                + _mm(jnp.where(incl_bd2, grams[p][pw:, :], 0.0), jnp.concatenate([u_st[p], vvs[p]], axis=0),
                      "nn", md["apply"]) for p in pairs]
        upd = [_mm(jnp.concatenate([unstack_heads(u_st[p]), vs[p]], axis=0), ak_tail[p], "tn", md["state"])
               for p in pairs]
        for p in pairs:
            state_ref[p] = state_ref[p] * e_tot[p] + same_head * upd[p]

        inv_n = 1.0 / hd
        ys = [unstack_heads(y) for y in y_st]
        means = [_mm(y, same_head, "nn", md["norm"]) * inv_n for y in ys]
        ycs = [y - mu_ for y, mu_ in zip(ys, means)]
        variances = [_mm(yc * yc, same_head, "nn", md["norm"]) * inv_n for yc in ycs]
        bonus = [_mm(r_all[:, pcols[p]] * ks[p] * par(_P_RK, p), same_head, "nn", md["norm"]) * vs[p] for p in pairs]
        for p in pairs:
            yn = ycs[p] * lax.rsqrt(variances[p] + RWKV_GN_EPS) * par(_P_LNW, p) + par(_P_LNB, p)
            o_ref[rows, pcols[p]] = ((yn + bonus[p]) * _silu(z_all[:, pcols[p]])).astype(o_ref.dtype)


def _rwkv(proj, params, mu_lora, w2, a2, *, batch, seq, tb):
    m = batch * seq
    nb = seq // tb
    w = RWKV_WIDTH
    base = COL_RWKV // w
    kern = functools.partial(_rwkv_kernel, nchunk=tb // RWKV_CHUNK)
    tok = lambda off: pl.BlockSpec((tb, w), lambda b, j, off=off: (b * nb + j, base + off))
    const = lambda shape: pl.BlockSpec(shape, lambda b, j: (0, 0))
    return pl.pallas_call(
        kern,
        grid=(batch, nb),
        in_specs=[const((_P_ROWS, w)), const((1, LANES)), const((RWKV_LORA, w)), const((RWKV_LORA, w)),
                  tok(0), tok(1), tok(2), tok(3),
                  pl.BlockSpec((tb, LANES), lambda b, j: (b * nb + j, COL_LORA // LANES))],
        out_specs=pl.BlockSpec((tb, w), lambda b, j: (b * nb + j, 0)),
        out_shape=jax.ShapeDtypeStruct((m, w), BF16),
        scratch_shapes=[pltpu.VMEM((RWKV_PAIRS, 2 * RWKV_HEAD_DIM, 2 * RWKV_HEAD_DIM), F32),
                        pltpu.VMEM((8, w), F32),
                        pltpu.VMEM((8, LANES), F32)],
        compiler_params=pltpu.CompilerParams(
            dimension_semantics=("parallel", "arbitrary"), vmem_limit_bytes=VMEM_LIMIT),
        name="rwkv7",
    )(params, mu_lora, w2, a2, proj, proj, proj, proj, proj)


GDN_MODE = {"gram": "bf16", "inv": "bf16", "apply": "bf16", "state": "bf16"}


def _gdn_kernel(conv_ref, alog_ref, dt_ref, nw_ref, q_ref, k_ref, v_ref, z_ref, ab_ref, o_ref,
                state_ref, prev_ref, *, nchunk):
    c = GDN_CHUNK
    dh = GDN_HEAD_DIM
    gw = GDN_WIDTH
    md = GDN_MODE

    @pl.when(pl.program_id(1) == 0)
    def _():
        state_ref[...] = jnp.zeros_like(state_ref)
        prev_ref[...] = jnp.zeros_like(prev_ref)

    row = lax.broadcasted_iota(jnp.int32, (c, c), 0)
    col = lax.broadcasted_iota(jnp.int32, (c, c), 1)
    tril_incl = jnp.where(row >= col, 1.0, 0.0).astype(F32)
    r2 = lax.broadcasted_iota(jnp.int32, (2 * c, 2 * c), 0)
    c2 = lax.broadcasted_iota(jnp.int32, (2 * c, 2 * c), 1)
    same_blk = (r2 < c) == (c2 < c)
    strict_bd = jnp.logical_and(same_blk, r2 > c2)
    incl_bd = jnp.logical_and(same_blk, r2 >= c2)

    def conv_silu(ref, slot, rows, cols):
        cur = ref[rows, cols]
        ext = jnp.concatenate([prev_ref[slot, :, cols], cur], axis=0)
        wcols = slice(slot * gw + cols.start, slot * gw + cols.stop)
        acc = cur * conv_ref[GDN_CONV - 1:GDN_CONV, wcols]
        for s in range(1, GDN_CONV):
            shifted = pltpu.roll(ext, s, 0)[8:, :]
            acc = acc + shifted * conv_ref[GDN_CONV - 1 - s:GDN_CONV - s, wcols]
        prev_ref[slot, :, cols] = cur[c - 8:, :]
        return _silu(acc)

    for ci in range(nchunk):
        rows = slice(ci * c, (ci + 1) * c)
        ab = ab_ref[rows, :]
        g_all = -jnp.exp(alog_ref[...]) * _softplus(ab + dt_ref[...])
        beta_all = _sigmoid(ab)
        gc_all = _mm_exact_lhs(tril_incl, g_all)
        gc_rows = gc_all.T
        qs, ks, kbs, vbs, kbe, qe, ktail, gcols, grows, lasts = [], [], [], [], [], [], [], [], [], []
        for h in range(GDN_HEADS):
            cols = slice(h * dh, (h + 1) * dh)
            gcol = gc_all[:, h:h + 1]
            beta = beta_all[:, GDN_HEADS + h:GDN_HEADS + h + 1]
            q = conv_silu(q_ref, 0, rows, cols)
            k = conv_silu(k_ref, 1, rows, cols)
            v = conv_silu(v_ref, 2, rows, cols)
            q = q * lax.rsqrt(jnp.sum(q * q, axis=-1, keepdims=True) + 1e-6) * (dh ** -0.5)
            k = k * lax.rsqrt(jnp.sum(k * k, axis=-1, keepdims=True) + 1e-6)
            kb = k * beta
            eg = jnp.exp(gcol)
            glast = gcol[c - 1:c, :]
            qs.append(q)
            ks.append(k)
            kbs.append(kb)
            vbs.append(v * beta)
            kbe.append(kb * eg)
            qe.append(q * eg)
            ktail.append(k * jnp.exp(glast - gcol))
            gcols.append(gcol)
            grows.append(gc_rows[h:h + 1, :])
            lasts.append(jnp.exp(glast))
        pairs = range(GDN_HEADS // 2)
        heads = [(2 * hp, 2 * hp + 1) for hp in pairs]
        grams = [_mm(jnp.concatenate([kbs[h0], kbs[h1], qs[h0], qs[h1]], axis=0),
                     jnp.concatenate([ks[h0], ks[h1]], axis=0), "nt", md["gram"]) for h0, h1 in heads]
        decays = []
        for h0, h1 in heads:
            diff = jnp.concatenate([gcols[h0], gcols[h1]], axis=0) - jnp.concatenate([grows[h0], grows[h1]], axis=1)
            decays.append(jnp.where(incl_bd, jnp.exp(jnp.minimum(diff, 0.0)), 0.0))
        t_bd = _block_unit_lower_inverses(
            [jnp.where(strict_bd, -(grams[hp][:2 * c] * decays[hp]), 0.0) for hp in pairs], c, md["inv"])
        sols = [_mm(t_bd[hp],
                    jnp.concatenate([jnp.concatenate([vbs[h0], kbe[h0]], axis=1),
                                     jnp.concatenate([vbs[h1], kbe[h1]], axis=1)], axis=0),
                    "nn", md["apply"]) for hp, (h0, h1) in enumerate(heads)]
        sq = [_mm(jnp.concatenate([sols[h // 2][(h % 2) * c:(h % 2 + 1) * c, dh:], qe[h]], axis=0), state_ref[h],
                  "nn", md["state"]) for h in range(GDN_HEADS)]
        v_new = [sols[hp][:, :dh] - jnp.concatenate([sq[h0][:c], sq[h1][:c]], axis=0)
                 for hp, (h0, h1) in enumerate(heads)]
        o_st = [jnp.concatenate([sq[h0][c:], sq[h1][c:]], axis=0)
                + _mm(grams[hp][2 * c:] * decays[hp], v_new[hp], "nn", md["apply"])
                for hp, (h0, h1) in enumerate(heads)]
        upd = [_mm(ktail[h], v_new[h // 2][(h % 2) * c:(h % 2 + 1) * c], "tn", md["state"]) for h in range(GDN_HEADS)]
        for h in range(GDN_HEADS):
            cols = slice(h * dh, (h + 1) * dh)
            state_ref[h] = state_ref[h] * lasts[h] + upd[h]
            o = o_st[h // 2][(h % 2) * c:(h % 2 + 1) * c]
            o = o * lax.rsqrt(jnp.mean(o * o, axis=-1, keepdims=True) + RMS_EPS) * nw_ref[...]
            o_ref[rows, cols] = (o * _silu(z_ref[rows, cols])).astype(o_ref.dtype)


def _gdn(proj, conv_w, alog_pad, dt_pad, norm_w, *, batch, seq, tb):
    m = batch * seq
    nb = seq // tb
    w = GDN_WIDTH
    base = COL_GDN // w
    kern = functools.partial(_gdn_kernel, nchunk=tb // GDN_CHUNK)
    tok = lambda off: pl.BlockSpec((tb, w), lambda b, j, off=off: (b * nb + j, base + off))
    const = lambda shape: pl.BlockSpec(shape, lambda b, j: (0, 0))
    return pl.pallas_call(
        kern,
        grid=(batch, nb),
        in_specs=[const((GDN_CONV, 3 * w)), const((1, LANES)), const((1, LANES)), const((1, GDN_HEAD_DIM)),
                  tok(0), tok(1), tok(2), tok(3),
                  pl.BlockSpec((tb, LANES), lambda b, j: (b * nb + j, COL_AB // LANES))],
        out_specs=pl.BlockSpec((tb, w), lambda b, j: (b * nb + j, 0)),
        out_shape=jax.ShapeDtypeStruct((m, w), BF16),
        scratch_shapes=[pltpu.VMEM((GDN_HEADS, GDN_HEAD_DIM, GDN_HEAD_DIM), F32),
                        pltpu.VMEM((3, 8, w), F32)],
        compiler_params=pltpu.CompilerParams(
            dimension_semantics=("parallel", "arbitrary"), vmem_limit_bytes=VMEM_LIMIT),
        name="gdn",
    )(conv_w, alog_pad, dt_pad, norm_w.reshape(1, GDN_HEAD_DIM), proj, proj, proj, proj, proj)


def _out_proj_kernel(x_ref, y1_ref, y2_ref, y3_ref, w1_ref, w2_ref, w3_ref, o_ref):
    acc = jnp.dot(y1_ref[...], w1_ref[...], preferred_element_type=F32)
    acc = acc + jnp.dot(y2_ref[...], w2_ref[...], preferred_element_type=F32)
    acc = acc + jnp.dot(y3_ref[...], w3_ref[...], preferred_element_type=F32)
    o_ref[...] = x_ref[...] + acc


def _out_proj(x2d, y_ret, y_rwkv, y_gdn, w_bf16, *, tm, tn):
    m, d = x2d.shape
    k1, k2, k3 = y_ret.shape[1], y_rwkv.shape[1], y_gdn.shape[1]
    return pl.pallas_call(
        _out_proj_kernel,
        grid=(m // tm, d // tn),
        in_specs=[pl.BlockSpec((tm, tn), lambda i, j: (i, j)),
                  pl.BlockSpec((tm, k1), lambda i, j: (i, 0)),
                  pl.BlockSpec((tm, k2), lambda i, j: (i, 0)),
                  pl.BlockSpec((tm, k3), lambda i, j: (i, 0)),
                  pl.BlockSpec((k1, tn), lambda i, j: (0, j)),
                  pl.BlockSpec((k2, tn), lambda i, j: (1, j)),
                  pl.BlockSpec((k3, tn), lambda i, j: (1, j))],
        out_specs=pl.BlockSpec((tm, tn), lambda i, j: (i, j)),
        out_shape=jax.ShapeDtypeStruct((m, d), F32),
        compiler_params=pltpu.CompilerParams(
            dimension_semantics=("parallel", "arbitrary"), vmem_limit_bytes=VMEM_LIMIT),
        name="out_proj",
    )(x2d, y_ret, y_rwkv, y_gdn, w_bf16, w_bf16, w_bf16)


def _ple_kernel(x_ref, nw_ref, p_ref, wp_ref, wg_ref, o_ref, h_ref, *, tn):
    j = pl.program_id(1)

    @pl.when(j == 0)
    def _():
        h_ref[...] = _rms_rows(x_ref[...], nw_ref[...]).astype(BF16)

    gate = _sigmoid(jnp.dot(h_ref[...], wg_ref[...], preferred_element_type=F32))
    ple = jnp.dot(p_ref[...].astype(BF16), wp_ref[...], preferred_element_type=F32)
    cols = pl.ds(pl.multiple_of(j * tn, tn), tn)
    o_ref[...] = x_ref[:, cols] + ple * gate


def _ple(x2d, norm_w, p2d, w_ple_bf16, w_gate_bf16, *, tm, tn):
    m, d = x2d.shape
    pd = p2d.shape[1]
    kern = functools.partial(_ple_kernel, tn=tn)
    return pl.pallas_call(
        kern,
        grid=(m // tm, d // tn),
        in_specs=[pl.BlockSpec((tm, d), lambda i, j: (i, 0)),
                  pl.BlockSpec((1, d), lambda i, j: (0, 0)),
                  pl.BlockSpec((tm, pd), lambda i, j: (i, 0)),
                  pl.BlockSpec((pd, tn), lambda i, j: (0, j)),
                  pl.BlockSpec((d, tn), lambda i, j: (0, j))],
        out_specs=pl.BlockSpec((tm, tn), lambda i, j: (i, j)),
        out_shape=jax.ShapeDtypeStruct((m, d), F32),
        scratch_shapes=[pltpu.VMEM((tm, d), BF16)],
        compiler_params=pltpu.CompilerParams(
            dimension_semantics=("parallel", "arbitrary"), vmem_limit_bytes=VMEM_LIMIT),
        name="ple_gate",
    )(x2d, norm_w.reshape(1, d), p2d, w_ple_bf16, w_gate_bf16)


def _final_norm_kernel(x_ref, w_ref, o_ref):
    o_ref[...] = _rms_rows(x_ref[...], w_ref[...])


def _final_norm(x2d, w, *, tm):
    m, d = x2d.shape
    return pl.pallas_call(
        _final_norm_kernel,
        grid=(m // tm,),
        in_specs=[pl.BlockSpec((tm, d), lambda i: (i, 0)), pl.BlockSpec((1, d), lambda i: (0, 0))],
        out_specs=pl.BlockSpec((tm, d), lambda i: (i, 0)),
        out_shape=jax.ShapeDtypeStruct((m, d), F32),
        compiler_params=pltpu.CompilerParams(
            dimension_semantics=("parallel",), vmem_limit_bytes=VMEM_LIMIT),
        name="final_norm",
    )(x2d, w.reshape(1, d))


def _reorder_in_weight(w):
    rwkv_lora0 = 4096 + 4 * RWKV_WIDTH
    gdn0 = rwkv_lora0 + 2 * RWKV_LORA
    ab0 = gdn0 + 4 * GDN_WIDTH
    pad = jnp.zeros((w.shape[0], IN_PAD - IN_WIDTH), w.dtype)
    out = jnp.concatenate([w[:, :rwkv_lora0], w[:, gdn0:ab0], w[:, rwkv_lora0:gdn0], w[:, ab0:], pad], axis=1)
    return out.astype(BF16)


def _pad_lanes(v):
    return jnp.pad(v.astype(F32), (0, LANES - v.shape[0])).reshape(1, LANES)


def kernel(x, p, positions, norm_w, w_in, ret_gn, rwkv_mu, rwkv_w0, rwkv_w2, rwkv_a0, rwkv_a2, rwkv_k_k, rwkv_k_a, rwkv_r_k, rwkv_ln_w, rwkv_ln_b, gdn_conv, gdn_a_log, gdn_dt_bias, gdn_norm, w_out, w_ple, ple_norm, w_ple_gate, final_norm):
    batch, seq, d = x.shape
    depth = w_in.shape[0]
    m = batch * seq
    tm = min(512, m)
    tb_ret = min(512, seq)
    tb_rwkv = min(64, seq)
    tb_gdn = min(64, seq)

    half = RET_HEAD_DIM // 2
    inv_freq = ROPE_BASE ** (-jnp.arange(half, dtype=F32) / half)
    inv_freq = jnp.concatenate([inv_freq, inv_freq]).reshape(1, RET_HEAD_DIM)
    pos2d = positions.reshape(m, 1)

    xc = x.reshape(m, d)
    for i in range(depth):
        proj = _in_proj(xc, norm_w[i], _reorder_in_weight(w_in[i]), tm=tm, tn=640)
        y_ret = _retention(proj, pos2d, inv_freq, ret_gn[i], batch=batch, seq=seq, tb=tb_ret)

        mu = rwkv_mu[i]
        wd = RWKV_WIDTH
        rows = [mu[0:wd], mu[wd:2 * wd], mu[2 * wd:3 * wd], mu[3 * wd:4 * wd], rwkv_w0[i], rwkv_a0[i],
                rwkv_k_k[i], rwkv_k_a[i], rwkv_r_k[i], rwkv_ln_w[i], rwkv_ln_b[i]]
        params = jnp.concatenate([jnp.stack(rows), jnp.zeros((_P_ROWS - len(rows), wd), F32)], axis=0)
        mu_lora = mu[4 * wd:].reshape(1, LANES)
        y_rwkv = _rwkv(proj, params, mu_lora, rwkv_w2[i], rwkv_a2[i], batch=batch, seq=seq, tb=tb_rwkv)

        alog_pad = _pad_lanes(gdn_a_log[i])
        dt_pad = _pad_lanes(gdn_dt_bias[i])
        y_gdn = _gdn(proj, gdn_conv[i].astype(F32), alog_pad, dt_pad, gdn_norm[i],
                     batch=batch, seq=seq, tb=tb_gdn)

        xc = _out_proj(xc, y_ret, y_rwkv, y_gdn, w_out[i].astype(BF16), tm=tm, tn=1024)
        xc = _ple(xc, ple_norm[i], p[i].reshape(m, PLE_DIM), w_ple[i].astype(BF16),
                  w_ple_gate[i].astype(BF16), tm=tm, tn=512)
    return _final_norm(xc, final_norm, tm=tm).reshape(batch, seq, d)
```

```python
import functools
import math

import jax
import jax.numpy as jnp
from jax import lax
from jax.experimental import pallas as pl
from jax.experimental.pallas import tpu as pltpu

F32 = jnp.float32
BF16 = jnp.bfloat16

LANES = 128
VMEM_LIMIT = 56 * 1024 * 1024

D_MODEL = 4096
PLE_DIM = 256
RMS_EPS = 1e-6

RET_HEAD_DIM = 128
RET_WIDTH = 1024
RET_HEADS = RET_WIDTH // RET_HEAD_DIM
RET_CHUNK = 128
ROPE_BASE = 10000.0
RET_GN_EPS = 1e-5

RWKV_HEAD_DIM = 64
RWKV_WIDTH = 1024
RWKV_HEADS = RWKV_WIDTH // RWKV_HEAD_DIM
RWKV_PAIRS = RWKV_HEADS // 2
RWKV_LORA = 64
RWKV_GN_EPS = 64e-5
RWKV_CHUNK = 64

GDN_HEAD_DIM = 128
GDN_WIDTH = 2048
GDN_HEADS = GDN_WIDTH // GDN_HEAD_DIM
GDN_CONV = 4
GDN_CHUNK = 64

COL_RET = 0
COL_RWKV = 4096
COL_GDN = 8192
COL_LORA = 16384
COL_AB = 16512
IN_PAD = 16640
IN_WIDTH = 16544


def _mm(a, b, dims, mode):
    dn = {"nn": (((1,), (0,)), ((), ())),
          "nt": (((1,), (1,)), ((), ())),
          "tn": (((0,), (0,)), ((), ()))}[dims]
    dot = functools.partial(lax.dot_general, dimension_numbers=dn, preferred_element_type=F32)
    if mode == "bf16":
        return dot(a.astype(BF16), b.astype(BF16))
    if mode == "x3":
        a_hi, b_hi = a.astype(BF16), b.astype(BF16)
        a_lo = (a - a_hi.astype(F32)).astype(BF16)
        b_lo = (b - b_hi.astype(F32)).astype(BF16)
        return dot(a_hi, b_hi) + (dot(a_hi, b_lo) + dot(a_lo, b_hi))
    return dot(a.astype(F32), b.astype(F32), precision=lax.Precision.HIGHEST)


def _mm_exact_lhs(a_exact, b):
    b1 = b.astype(BF16)
    r1 = b - b1.astype(F32)
    b2 = r1.astype(BF16)
    b3 = (r1 - b2.astype(F32)).astype(BF16)
    a16 = a_exact.astype(BF16)
    return _mm(a16, b1, "nn", "bf16") + (_mm(a16, b2, "nn", "bf16") + _mm(a16, b3, "nn", "bf16"))


def _softplus(x):
    return jnp.maximum(x, 0.0) + jnp.log1p(jnp.exp(-jnp.abs(x)))


def _sigmoid(x):
    return 1.0 / (1.0 + jnp.exp(-x))


def _silu(x):
    return x * _sigmoid(x)


def _block_unit_lower_inverses(n_mats, nilpotent_order, mode):
    n = n_mats[0].shape[0]
    row = lax.broadcasted_iota(jnp.int32, (n, n), 0)
    col = lax.broadcasted_iota(jnp.int32, (n, n), 1)
    eye = jnp.where(row == col, 1.0, 0.0).astype(F32)
    accs = [eye + m for m in n_mats]
    powers = [_mm(m, m, "nn", mode) for m in n_mats]
    steps = int(math.log2(nilpotent_order)) - 1
    for j in range(steps):
        if j == steps - 1:
            accs = [acc + _mm(acc, pw_, "nn", mode) for acc, pw_ in zip(accs, powers)]
        else:
            both = [_mm(jnp.concatenate([pw_, acc], axis=0), pw_, "nn", mode) for acc, pw_ in zip(accs, powers)]
            powers = [b[:n] for b in both]
            accs = [acc + b[n:] for acc, b in zip(accs, both)]
    return accs


def _rms_rows(x, w):
    return x * lax.rsqrt(jnp.mean(x * x, axis=-1, keepdims=True) + RMS_EPS) * w


def _rms_cast_kernel(x_ref, w_ref, o_ref):
    o_ref[...] = _rms_rows(x_ref[...], w_ref[...]).astype(o_ref.dtype)


def _rms_cast(x2d, w, *, tm):
    m, d = x2d.shape
    return pl.pallas_call(
        _rms_cast_kernel,
        grid=(m // tm,),
        in_specs=[pl.BlockSpec((tm, d), lambda i: (i, 0)), pl.BlockSpec((1, d), lambda i: (0, 0))],
        out_specs=pl.BlockSpec((tm, d), lambda i: (i, 0)),
        out_shape=jax.ShapeDtypeStruct((m, d), BF16),
        compiler_params=pltpu.CompilerParams(
            dimension_semantics=("parallel",), vmem_limit_bytes=VMEM_LIMIT),
        name="rms_cast",
    )(x2d, w.reshape(1, d))


def _in_proj_kernel(h_ref, w_ref, o_ref):
    o_ref[...] = jnp.dot(h_ref[...], w_ref[...], preferred_element_type=F32)


def _in_proj(h2d, w_bf16, *, tm, tn):
    m, d = h2d.shape
    n = w_bf16.shape[1]
    return pl.pallas_call(
        _in_proj_kernel,
        grid=(m // tm, n // tn),
        in_specs=[pl.BlockSpec((tm, d), lambda i, j: (i, 0)),
                  pl.BlockSpec((d, tn), lambda i, j: (0, j))],
        out_specs=pl.BlockSpec((tm, tn), lambda i, j: (i, j)),
        out_shape=jax.ShapeDtypeStruct((m, n), F32),
        compiler_params=pltpu.CompilerParams(
            dimension_semantics=("parallel", "arbitrary"), vmem_limit_bytes=VMEM_LIMIT),
        name="in_proj",
    )(h2d, w_bf16)


def _ret_kernel(pos_ref, invf_ref, gn_ref, q_ref, k_ref, v_ref, z_ref, o_ref, state_ref, *, nchunk):
    c = RET_CHUNK
    dh = RET_HEAD_DIM

    @pl.when(pl.program_id(1) == 0)
    def _():
        state_ref[...] = jnp.zeros_like(state_ref)

    row = lax.broadcasted_iota(jnp.int32, (c, c), 0)
    col = lax.broadcasted_iota(jnp.int32, (c, c), 1)
    rel = (row - col).astype(F32)
    idx = lax.broadcasted_iota(jnp.int32, (c, 1), 0).astype(F32)
    first_half = col < dh // 2

    for ci in range(nchunk):
        rows = slice(ci * c, (ci + 1) * c)
        ang = pos_ref[rows, :].astype(F32) * invf_ref[...]
        cos = jnp.cos(ang)
        sin = jnp.sin(ang)
        sin_signed = jnp.where(first_half, -sin, sin)
        heads = range(RET_HEADS)
        hcols = [slice(h * dh, (h + 1) * dh) for h in heads]
        log_gamma = [math.log1p(-(2.0 ** (-5.0 - h))) for h in heads]
        qr, kr, vs = [], [], []
        for h in heads:
            q = q_ref[rows, hcols[h]]
            k = k_ref[rows, hcols[h]]
            qr.append(q * cos + pltpu.roll(q, dh // 2, 1) * sin_signed)
            kr.append((k * cos + pltpu.roll(k, dh // 2, 1) * sin_signed) * (dh ** -0.5))
            vs.append(v_ref[rows, hcols[h]])
        scores = [_mm(qr[h], kr[h], "nt", "bf16")
                  * jnp.where(rel >= 0, jnp.exp(jnp.maximum(rel, 0.0) * log_gamma[h]), 0.0) for h in heads]
        cross = [_mm(qr[h] * jnp.exp((idx + 1.0) * log_gamma[h]), state_ref[h], "nn", "bf16") for h in heads]
        kv = [_mm(kr[h] * jnp.exp((c - 1.0 - idx) * log_gamma[h]), vs[h], "tn", "bf16") for h in heads]
        inner = [_mm(scores[h], vs[h], "nn", "bf16") for h in heads]
        for h in heads:
            state_ref[h] = state_ref[h] * math.exp(c * log_gamma[h]) + kv[h]
            y = inner[h] + cross[h]
            yc = y - jnp.mean(y, axis=-1, keepdims=True)
            yn = yc * lax.rsqrt(jnp.mean(yc * yc, axis=-1, keepdims=True) + RET_GN_EPS)
            o_ref[rows, hcols[h]] = (yn * gn_ref[:, hcols[h]] * _silu(z_ref[rows, hcols[h]])).astype(o_ref.dtype)


def _retention(proj, pos2d, inv_freq, gn_w, *, batch, seq, tb):
    m = batch * seq
    nb = seq // tb
    w = RET_WIDTH
    base = COL_RET // w
    kern = functools.partial(_ret_kernel, nchunk=tb // RET_CHUNK)
    tok = lambda off: pl.BlockSpec((tb, w), lambda b, j, off=off: (b * nb + j, base + off))
    return pl.pallas_call(
        kern,
        grid=(batch, nb),
        in_specs=[pl.BlockSpec((tb, 1), lambda b, j: (b * nb + j, 0)),
                  pl.BlockSpec((1, RET_HEAD_DIM), lambda b, j: (0, 0)),
                  pl.BlockSpec((1, w), lambda b, j: (0, 0)),
                  tok(0), tok(1), tok(2), tok(3)],
        out_specs=pl.BlockSpec((tb, w), lambda b, j: (b * nb + j, 0)),
        out_shape=jax.ShapeDtypeStruct((m, w), BF16),
        scratch_shapes=[pltpu.VMEM((RET_HEADS, RET_HEAD_DIM, RET_HEAD_DIM), F32)],
        compiler_params=pltpu.CompilerParams(
            dimension_semantics=("parallel", "arbitrary"), vmem_limit_bytes=VMEM_LIMIT),
        name="retention",
    )(pos2d, inv_freq, gn_w.reshape(1, w), proj, proj, proj, proj)


_P_MU_R, _P_MU_K, _P_MU_V, _P_MU_Z, _P_W0, _P_A0, _P_KK, _P_KA, _P_RK, _P_LNW, _P_LNB = range(11)
_P_ROWS = 16

RWKV_MODE = {"lora": "bf16", "gram": "bf16", "inv": "bf16", "apply": "bf16", "state": "bf16", "norm": "bf16"}


def _shift_rows(x, prev_row):
    rolled = pltpu.roll(x, 1, 0)
    first = lax.broadcasted_iota(jnp.int32, x.shape, 0) == 0
    return jnp.where(first, prev_row, rolled)


def _rwkv_kernel(par_ref, mul_ref, w2_ref, a2_ref, r_ref, k_ref, v_ref, z_ref, lo_ref, o_ref,
                 state_ref, prev_ref, prevlo_ref, *, nchunk):
    c = RWKV_CHUNK
    hd = RWKV_HEAD_DIM
    pw = 2 * hd
    md = RWKV_MODE

    @pl.when(pl.program_id(1) == 0)
    def _():
        state_ref[...] = jnp.zeros_like(state_ref)
        prev_ref[...] = jnp.zeros_like(prev_ref)
        prevlo_ref[...] = jnp.zeros_like(prevlo_ref)

    row = lax.broadcasted_iota(jnp.int32, (c, c), 0)
    col = lax.broadcasted_iota(jnp.int32, (c, c), 1)
    tril_incl = jnp.where(row >= col, 1.0, 0.0).astype(F32)
    lane = lax.broadcasted_iota(jnp.int32, (1, pw), 1)
    m0 = jnp.where(lane < hd, 1.0, 0.0).astype(F32)
    m1 = 1.0 - m0
    in_head0 = lax.broadcasted_iota(jnp.int32, (c, pw), 1) < hd
    r2 = lax.broadcasted_iota(jnp.int32, (pw, pw), 0)
    c2 = lax.broadcasted_iota(jnp.int32, (pw, pw), 1)
    same_blk = (r2 < hd) == (c2 < hd)
    same_head = jnp.where(same_blk, 1.0, 0.0).astype(F32)
    strict_bd = jnp.logical_and(same_blk, r2 > c2)
    incl_bd = jnp.logical_and(same_blk, r2 >= c2)
    incl_bd2 = jnp.concatenate([incl_bd, incl_bd], axis=1)

    def stack_heads(x):
        return jnp.concatenate([x * m0, x * m1], axis=0)

    def unstack_heads(x_st):
        return jnp.where(in_head0, x_st[:c], x_st[c:])

    def mixed(raw, prev_row, mu):
        return raw + (_shift_rows(raw, prev_row) - raw) * mu

    for ci in range(nchunk):
        rows = slice(ci * c, (ci + 1) * c)
        lo_raw = lo_ref[rows, :]
        lo = mixed(lo_raw, prevlo_ref[0:1, :], mul_ref[...])
        prevlo_ref[0:1, :] = lo_raw[c - 1:c, :]
        r_raw, k_raw, v_raw, z_raw = r_ref[rows, :], k_ref[rows, :], v_ref[rows, :], z_ref[rows, :]
        r_all = mixed(r_raw, prev_ref[0:1, :], par_ref[_P_MU_R:_P_MU_R + 1, :])
        k_all = mixed(k_raw, prev_ref[1:2, :], par_ref[_P_MU_K:_P_MU_K + 1, :])
        v_all = mixed(v_raw, prev_ref[2:3, :], par_ref[_P_MU_V:_P_MU_V + 1, :])
        z_all = mixed(z_raw, prev_ref[3:4, :], par_ref[_P_MU_Z:_P_MU_Z + 1, :])
        prev_ref[0:1, :] = r_raw[c - 1:c, :]
        prev_ref[1:2, :] = k_raw[c - 1:c, :]
        prev_ref[2:3, :] = v_raw[c - 1:c, :]
        prev_ref[3:4, :] = z_raw[c - 1:c, :]

        dw = jnp.tanh(lo[:, :RWKV_LORA])
        da = lo[:, RWKV_LORA:]
        w_log = -_softplus(-(par_ref[_P_W0:_P_W0 + 1, :] + _mm(dw, w2_ref[...], "nn", md["lora"]))) - 0.5
        logw_all = -jnp.exp(w_log)
        a_all = _sigmoid(par_ref[_P_A0:_P_A0 + 1, :] + _mm(da, a2_ref[...], "nn", md["lora"]))
        cs_all = _mm_exact_lhs(tril_incl, logw_all)

        pairs = range(RWKV_PAIRS)
        pcols = [slice(p * pw, (p + 1) * pw) for p in pairs]
        par = lambda i, p: par_ref[i:i + 1, pcols[p]]
        kk_raw = [k_all[:, pcols[p]] * par(_P_KK, p) for p in pairs]
        kk_ss = [_mm(x * x, same_head, "nn", md["norm"]) for x in kk_raw]
        br_st, ak_st, ak_tail, vs, ks, e_tot = [], [], [], [], [], []
        for p in pairs:
            cols = pcols[p]
            r, v, a = r_all[:, cols], v_all[:, cols], a_all[:, cols]
            logw, cs = logw_all[:, cols], cs_all[:, cols]
            kk = kk_raw[p] * lax.rsqrt(kk_ss[p] + 1e-6)
            k = k_all[:, cols] * (1.0 + (a - 1.0) * par(_P_KA, p))
            tot = cs[c - 1:c, :]
            e_neg = jnp.exp(-cs)
            e_tail = jnp.exp(tot - cs)
            neg_kka = -(kk * a)
            rt = r * jnp.exp(cs)
            bt = kk * jnp.exp(cs - logw)
            br_st.append(jnp.concatenate([stack_heads(bt), stack_heads(rt)], axis=0))
            ak_st.append(jnp.concatenate([stack_heads(neg_kka * e_neg), stack_heads(k * e_neg)], axis=0))
            ak_tail.append(jnp.concatenate([neg_kka * e_tail, k * e_tail], axis=0))
            vs.append(v)
            ks.append(k)
            e_tot.append(jnp.exp(tot))
        grams = [_mm(br_st[p], ak_st[p], "nt", md["gram"]) for p in pairs]
        t_bd = _block_unit_lower_inverses([jnp.where(strict_bd, g[:pw, :pw], 0.0) for g in grams], c, md["inv"])
        vvs = [jnp.concatenate([v, v], axis=0) for v in vs]
        x0 = [_mm(jnp.where(strict_bd, grams[p][:pw, pw:], 0.0), vvs[p], "nn", md["apply"]) for p in pairs]
        from_state = [_mm(br_st[p], state_ref[p], "nt", md["state"]) for p in pairs]
        u_st = [_mm(t_bd[p], from_state[p][:pw] + x0[p], "nn", md["apply"]) for p in pairs]
        y_st = [from_state[p][pw:]
                + _mm(jnp.where(incl_bd2, grams[p][pw:, :], 0.0), jnp.concatenate([u_st[p], vvs[p]], axis=0),
                      "nn", md["apply"]) for p in pairs]
        upd = [_mm(jnp.concatenate([unstack_heads(u_st[p]), vs[p]], axis=0), ak_tail[p], "tn", md["state"])
               for p in pairs]
        for p in pairs:
            state_ref[p] = state_ref[p] * e_tot[p] + same_head * upd[p]

        inv_n = 1.0 / hd
        ys = [unstack_heads(y) for y in y_st]
        means = [_mm(y, same_head, "nn", md["norm"]) * inv_n for y in ys]
        ycs = [y - mu_ for y, mu_ in zip(ys, means)]
        variances = [_mm(yc * yc, same_head, "nn", md["norm"]) * inv_n for yc in ycs]
        bonus = [_mm(r_all[:, pcols[p]] * ks[p] * par(_P_RK, p), same_head, "nn", md["norm"]) * vs[p] for p in pairs]
        for p in pairs:
            yn = ycs[p] * lax.rsqrt(variances[p] + RWKV_GN_EPS) * par(_P_LNW, p) + par(_P_LNB, p)
            o_ref[rows, pcols[p]] = ((yn + bonus[p]) * _silu(z_all[:, pcols[p]])).astype(o_ref.dtype)


def _rwkv(proj, params, mu_lora, w2, a2, *, batch, seq, tb):
    m = batch * seq
    nb = seq // tb
    w = RWKV_WIDTH
    base = COL_RWKV // w
    kern = functools.partial(_rwkv_kernel, nchunk=tb // RWKV_CHUNK)
    tok = lambda off: pl.BlockSpec((tb, w), lambda b, j, off=off: (b * nb + j, base + off))
    const = lambda shape: pl.BlockSpec(shape, lambda b, j: (0, 0))
    return pl.pallas_call(
        kern,
        grid=(batch, nb),
        in_specs=[const((_P_ROWS, w)), const((1, LANES)), const((RWKV_LORA, w)), const((RWKV_LORA, w)),
                  tok(0), tok(1), tok(2), tok(3),
                  pl.BlockSpec((tb, LANES), lambda b, j: (b * nb + j, COL_LORA // LANES))],
        out_specs=pl.BlockSpec((tb, w), lambda b, j: (b * nb + j, 0)),
        out_shape=jax.ShapeDtypeStruct((m, w), BF16),
        scratch_shapes=[pltpu.VMEM((RWKV_PAIRS, 2 * RWKV_HEAD_DIM, 2 * RWKV_HEAD_DIM), F32),
                        pltpu.VMEM((8, w), F32),
                        pltpu.VMEM((8, LANES), F32)],
        compiler_params=pltpu.CompilerParams(
            dimension_semantics=("parallel", "arbitrary"), vmem_limit_bytes=VMEM_LIMIT),
        name="rwkv7",
    )(params, mu_lora, w2, a2, proj, proj, proj, proj, proj)


GDN_MODE = {"gram": "bf16", "inv": "bf16", "apply": "bf16", "state": "bf16"}


def _gdn_kernel(conv_ref, alog_ref, dt_ref, nw_ref, q_ref, k_ref, v_ref, z_ref, ab_ref, o_ref,
                state_ref, prev_ref, *, nchunk):
    c = GDN_CHUNK
    dh = GDN_HEAD_DIM
    gw = GDN_WIDTH
    md = GDN_MODE

    @pl.when(pl.program_id(1) == 0)
    def _():
        state_ref[...] = jnp.zeros_like(state_ref)
        prev_ref[...] = jnp.zeros_like(prev_ref)

    row = lax.broadcasted_iota(jnp.int32, (c, c), 0)
    col = lax.broadcasted_iota(jnp.int32, (c, c), 1)
    tril_incl = jnp.where(row >= col, 1.0, 0.0).astype(F32)
    r2 = lax.broadcasted_iota(jnp.int32, (2 * c, 2 * c), 0)
    c2 = lax.broadcasted_iota(jnp.int32, (2 * c, 2 * c), 1)
    same_blk = (r2 < c) == (c2 < c)
    strict_bd = jnp.logical_and(same_blk, r2 > c2)
    incl_bd = jnp.logical_and(same_blk, r2 >= c2)

    def conv_silu(ref, slot, rows, cols):
        cur = ref[rows, cols]
        ext = jnp.concatenate([prev_ref[slot, :, cols], cur], axis=0)
        wcols = slice(slot * gw + cols.start, slot * gw + cols.stop)
        acc = cur * conv_ref[GDN_CONV - 1:GDN_CONV, wcols]
        for s in range(1, GDN_CONV):
            shifted = pltpu.roll(ext, s, 0)[8:, :]
            acc = acc + shifted * conv_ref[GDN_CONV - 1 - s:GDN_CONV - s, wcols]
        prev_ref[slot, :, cols] = cur[c - 8:, :]
        return _silu(acc)

    for ci in range(nchunk):
        rows = slice(ci * c, (ci + 1) * c)
        ab = ab_ref[rows, :]
        g_all = -jnp.exp(alog_ref[...]) * _softplus(ab + dt_ref[...])
        beta_all = _sigmoid(ab)
        gc_all = _mm_exact_lhs(tril_incl, g_all)
        gc_rows = gc_all.T
        qs, ks, kbs, vbs, kbe, qe, ktail, gcols, grows, lasts = [], [], [], [], [], [], [], [], [], []
        for h in range(GDN_HEADS):
            cols = slice(h * dh, (h + 1) * dh)
            gcol = gc_all[:, h:h + 1]
            beta = beta_all[:, GDN_HEADS + h:GDN_HEADS + h + 1]
            q = conv_silu(q_ref, 0, rows, cols)
            k = conv_silu(k_ref, 1, rows, cols)
            v = conv_silu(v_ref, 2, rows, cols)
            q = q * lax.rsqrt(jnp.sum(q * q, axis=-1, keepdims=True) + 1e-6) * (dh ** -0.5)
            k = k * lax.rsqrt(jnp.sum(k * k, axis=-1, keepdims=True) + 1e-6)
            kb = k * beta
            eg = jnp.exp(gcol)
            glast = gcol[c - 1:c, :]
            qs.append(q)
            ks.append(k)
            kbs.append(kb)
            vbs.append(v * beta)
            kbe.append(kb * eg)
            qe.append(q * eg)
            ktail.append(k * jnp.exp(glast - gcol))
            gcols.append(gcol)
            grows.append(gc_rows[h:h + 1, :])
            lasts.append(jnp.exp(glast))
        pairs = range(GDN_HEADS // 2)
        heads = [(2 * hp, 2 * hp + 1) for hp in pairs]
        grams = [_mm(jnp.concatenate([kbs[h0], kbs[h1], qs[h0], qs[h1]], axis=0),
                     jnp.concatenate([ks[h0], ks[h1]], axis=0), "nt", md["gram"]) for h0, h1 in heads]
        decays = []
        for h0, h1 in heads:
            diff = jnp.concatenate([gcols[h0], gcols[h1]], axis=0) - jnp.concatenate([grows[h0], grows[h1]], axis=1)
            decays.append(jnp.where(incl_bd, jnp.exp(jnp.minimum(diff, 0.0)), 0.0))
        t_bd = _block_unit_lower_inverses(
            [jnp.where(strict_bd, -(grams[hp][:2 * c] * decays[hp]), 0.0) for hp in pairs], c, md["inv"])
        sols = [_mm(t_bd[hp],
                    jnp.concatenate([jnp.concatenate([vbs[h0], kbe[h0]], axis=1),
                                     jnp.concatenate([vbs[h1], kbe[h1]], axis=1)], axis=0),
                    "nn", md["apply"]) for hp, (h0, h1) in enumerate(heads)]
        sq = [_mm(jnp.concatenate([sols[h // 2][(h % 2) * c:(h % 2 + 1) * c, dh:], qe[h]], axis=0), state_ref[h],
                  "nn", md["state"]) for h in range(GDN_HEADS)]
        v_new = [sols[hp][:, :dh] - jnp.concatenate([sq[h0][:c], sq[h1][:c]], axis=0)
                 for hp, (h0, h1) in enumerate(heads)]
        o_st = [jnp.concatenate([sq[h0][c:], sq[h1][c:]], axis=0)
                + _mm(grams[hp][2 * c:] * decays[hp], v_new[hp], "nn", md["apply"])
                for hp, (h0, h1) in enumerate(heads)]
        upd = [_mm(ktail[h], v_new[h // 2][(h % 2) * c:(h % 2 + 1) * c], "tn", md["state"]) for h in range(GDN_HEADS)]
        for h in range(GDN_HEADS):
            cols = slice(h * dh, (h + 1) * dh)
            state_ref[h] = state_ref[h] * lasts[h] + upd[h]
            o = o_st[h // 2][(h % 2) * c:(h % 2 + 1) * c]
            o = o * lax.rsqrt(jnp.mean(o * o, axis=-1, keepdims=True) + RMS_EPS) * nw_ref[...]
            o_ref[rows, cols] = (o * _silu(z_ref[rows, cols])).astype(o_ref.dtype)


def _gdn(proj, conv_w, alog_pad, dt_pad, norm_w, *, batch, seq, tb):
    m = batch * seq
    nb = seq // tb
    w = GDN_WIDTH
    base = COL_GDN // w
    kern = functools.partial(_gdn_kernel, nchunk=tb // GDN_CHUNK)
    tok = lambda off: pl.BlockSpec((tb, w), lambda b, j, off=off: (b * nb + j, base + off))
    const = lambda shape: pl.BlockSpec(shape, lambda b, j: (0, 0))
    return pl.pallas_call(
        kern,
        grid=(batch, nb),
        in_specs=[const((GDN_CONV, 3 * w)), const((1, LANES)), const((1, LANES)), const((1, GDN_HEAD_DIM)),
                  tok(0), tok(1), tok(2), tok(3),
                  pl.BlockSpec((tb, LANES), lambda b, j: (b * nb + j, COL_AB // LANES))],
        out_specs=pl.BlockSpec((tb, w), lambda b, j: (b * nb + j, 0)),
        out_shape=jax.ShapeDtypeStruct((m, w), BF16),
        scratch_shapes=[pltpu.VMEM((GDN_HEADS, GDN_HEAD_DIM, GDN_HEAD_DIM), F32),
                        pltpu.VMEM((3, 8, w), F32)],
        compiler_params=pltpu.CompilerParams(
            dimension_semantics=("parallel", "arbitrary"), vmem_limit_bytes=VMEM_LIMIT),
        name="gdn",
    )(conv_w, alog_pad, dt_pad, norm_w.reshape(1, GDN_HEAD_DIM), proj, proj, proj, proj, proj)


def _out_proj_kernel(x_ref, y1_ref, y2_ref, y3_ref, w1_ref, w2_ref, w3_ref, o_ref):
    acc = jnp.dot(y1_ref[...], w1_ref[...], preferred_element_type=F32)
    acc = acc + jnp.dot(y2_ref[...], w2_ref[...], preferred_element_type=F32)
    acc = acc + jnp.dot(y3_ref[...], w3_ref[...], preferred_element_type=F32)
    o_ref[...] = x_ref[...] + acc


def _out_proj(x2d, y_ret, y_rwkv, y_gdn, w_bf16, *, tm, tn):
    m, d = x2d.shape
    k1, k2, k3 = y_ret.shape[1], y_rwkv.shape[1], y_gdn.shape[1]
    return pl.pallas_call(
        _out_proj_kernel,
        grid=(m // tm, d // tn),
        in_specs=[pl.BlockSpec((tm, tn), lambda i, j: (i, j)),
                  pl.BlockSpec((tm, k1), lambda i, j: (i, 0)),
                  pl.BlockSpec((tm, k2), lambda i, j: (i, 0)),
                  pl.BlockSpec((tm, k3), lambda i, j: (i, 0)),
                  pl.BlockSpec((k1, tn), lambda i, j: (0, j)),
                  pl.BlockSpec((k2, tn), lambda i, j: (1, j)),
                  pl.BlockSpec((k3, tn), lambda i, j: (1, j))],
        out_specs=pl.BlockSpec((tm, tn), lambda i, j: (i, j)),
        out_shape=jax.ShapeDtypeStruct((m, d), F32),
        compiler_params=pltpu.CompilerParams(
            dimension_semantics=("parallel", "arbitrary"), vmem_limit_bytes=VMEM_LIMIT),
        name="out_proj",
    )(x2d, y_ret, y_rwkv, y_gdn, w_bf16, w_bf16, w_bf16)


def _ple_kernel(x_ref, nw_ref, p_ref, wp_ref, wg_ref, o_ref, h_ref, *, tn):
    j = pl.program_id(1)

    @pl.when(j == 0)
    def _():
        h_ref[...] = _rms_rows(x_ref[...], nw_ref[...]).astype(BF16)

    gate = _sigmoid(jnp.dot(h_ref[...], wg_ref[...], preferred_element_type=F32))
    ple = jnp.dot(p_ref[...].astype(BF16), wp_ref[...], preferred_element_type=F32)
    cols = pl.ds(pl.multiple_of(j * tn, tn), tn)
    o_ref[...] = x_ref[:, cols] + ple * gate


def _ple(x2d, norm_w, p2d, w_ple_bf16, w_gate_bf16, *, tm, tn):
    m, d = x2d.shape
    pd = p2d.shape[1]
    kern = functools.partial(_ple_kernel, tn=tn)
    return pl.pallas_call(
        kern,
        grid=(m // tm, d // tn),
        in_specs=[pl.BlockSpec((tm, d), lambda i, j: (i, 0)),
                  pl.BlockSpec((1, d), lambda i, j: (0, 0)),
                  pl.BlockSpec((tm, pd), lambda i, j: (i, 0)),
                  pl.BlockSpec((pd, tn), lambda i, j: (0, j)),
                  pl.BlockSpec((d, tn), lambda i, j: (0, j))],
        out_specs=pl.BlockSpec((tm, tn), lambda i, j: (i, j)),
        out_shape=jax.ShapeDtypeStruct((m, d), F32),
        scratch_shapes=[pltpu.VMEM((tm, d), BF16)],
        compiler_params=pltpu.CompilerParams(
            dimension_semantics=("parallel", "arbitrary"), vmem_limit_bytes=VMEM_LIMIT),
        name="ple_gate",
    )(x2d, norm_w.reshape(1, d), p2d, w_ple_bf16, w_gate_bf16)


def _final_norm_kernel(x_ref, w_ref, o_ref):
    o_ref[...] = _rms_rows(x_ref[...], w_ref[...])


def _final_norm(x2d, w, *, tm):
    m, d = x2d.shape
    return pl.pallas_call(
        _final_norm_kernel,
        grid=(m // tm,),
        in_specs=[pl.BlockSpec((tm, d), lambda i: (i, 0)), pl.BlockSpec((1, d), lambda i: (0, 0))],
        out_specs=pl.BlockSpec((tm, d), lambda i: (i, 0)),
        out_shape=jax.ShapeDtypeStruct((m, d), F32),
        compiler_params=pltpu.CompilerParams(
            dimension_semantics=("parallel",), vmem_limit_bytes=VMEM_LIMIT),
        name="final_norm",
    )(x2d, w.reshape(1, d))


def _reorder_in_weight(w):
    rwkv_lora0 = 4096 + 4 * RWKV_WIDTH
    gdn0 = rwkv_lora0 + 2 * RWKV_LORA
    ab0 = gdn0 + 4 * GDN_WIDTH
    pad = jnp.zeros((w.shape[0], IN_PAD - IN_WIDTH), BF16)
    parts = [w[:, :rwkv_lora0], w[:, gdn0:ab0], w[:, rwkv_lora0:gdn0], w[:, ab0:]]
    return jnp.concatenate([part.astype(BF16) for part in parts] + [pad], axis=1)


def _pad_lanes(v):
    return jnp.pad(v.astype(F32), (0, LANES - v.shape[0])).reshape(1, LANES)


def kernel(x, p, positions, norm_w, w_in, ret_gn, rwkv_mu, rwkv_w0, rwkv_w2, rwkv_a0, rwkv_a2, rwkv_k_k, rwkv_k_a, rwkv_r_k, rwkv_ln_w, rwkv_ln_b, gdn_conv, gdn_a_log, gdn_dt_bias, gdn_norm, w_out, w_ple, ple_norm, w_ple_gate, final_norm):
    batch, seq, d = x.shape
    depth = w_in.shape[0]
    m = batch * seq
    tm = min(512, m)
    tb_ret = min(512, seq)
    tb_rwkv = min(128, seq)
    tb_gdn = min(128, seq)

    half = RET_HEAD_DIM // 2
    inv_freq = ROPE_BASE ** (-jnp.arange(half, dtype=F32) / half)
    inv_freq = jnp.concatenate([inv_freq, inv_freq]).reshape(1, RET_HEAD_DIM)
    pos2d = positions.reshape(m, 1)

    xc = x.reshape(m, d)
    for i in range(depth):
        proj = _in_proj(_rms_cast(xc, norm_w[i], tm=tm), _reorder_in_weight(w_in[i]),
                        tm=min(1024, m), tn=640)
        y_ret = _retention(proj, pos2d, inv_freq, ret_gn[i], batch=batch, seq=seq, tb=tb_ret)

        mu = rwkv_mu[i]
        wd = RWKV_WIDTH
        rows = [mu[0:wd], mu[wd:2 * wd], mu[2 * wd:3 * wd], mu[3 * wd:4 * wd], rwkv_w0[i], rwkv_a0[i],
                rwkv_k_k[i], rwkv_k_a[i], rwkv_r_k[i], rwkv_ln_w[i], rwkv_ln_b[i]]
        params = jnp.concatenate([jnp.stack(rows), jnp.zeros((_P_ROWS - len(rows), wd), F32)], axis=0)
        mu_lora = mu[4 * wd:].reshape(1, LANES)
        y_rwkv = _rwkv(proj, params, mu_lora, rwkv_w2[i], rwkv_a2[i], batch=batch, seq=seq, tb=tb_rwkv)

        alog_pad = _pad_lanes(gdn_a_log[i])
        dt_pad = _pad_lanes(gdn_dt_bias[i])
        y_gdn = _gdn(proj, gdn_conv[i].astype(F32), alog_pad, dt_pad, gdn_norm[i],
                     batch=batch, seq=seq, tb=tb_gdn)

        xc = _out_proj(xc, y_ret, y_rwkv, y_gdn, w_out[i].astype(BF16), tm=tm, tn=1024)
        xc = _ple(xc, ple_norm[i], p[i].reshape(m, PLE_DIM), w_ple[i].astype(BF16),
                  w_ple_gate[i].astype(BF16), tm=tm, tn=512)
    return _final_norm(xc, final_norm, tm=tm).reshape(batch, seq, d)
```
